```python
import jax, jax.numpy as jnp
from jax import lax
import numpy as np

D_MODEL = 2048
BATCH = 8
SEQ = 4096
DEPTH = 1

RET_HEADS = 8
RET_HEAD_DIM = 128
RET_WIDTH = RET_HEADS * RET_HEAD_DIM
RET_CHUNK = 128
MLA_HEADS = 8
MLA_NOPE_DIM = 128
MLA_ROPE_DIM = 64
MLA_V_DIM = 128
MLA_Q_RANK = 512
MLA_KV_RANK = 512
MLA_WIDTH = MLA_HEADS * MLA_V_DIM
MLA_QK_DIM = MLA_NOPE_DIM + MLA_ROPE_DIM
Q_BLOCK = 128
MIX_WIDTH = RET_WIDTH + MLA_WIDTH
IN_COLS = 4 * RET_WIDTH + MLA_Q_RANK + MLA_KV_RANK + MLA_ROPE_DIM
D_FF = -(-(8 * D_MODEL) // (3 * 256)) * 256
N_MOD = 6
ROPE_BASE = 10000.0
EPS = 1e-6

kernel_name = "hybrid_retention_mla_adaln_block"


def rmsnorm(x, w):
    xf = x.astype(jnp.float32)
    y = xf * lax.rsqrt(jnp.mean(xf * xf, axis=-1, keepdims=True) + EPS)
    return (y * w.astype(jnp.float32)).astype(x.dtype)


def rope_tables(positions, dim):
    inv_freq = ROPE_BASE ** (-jnp.arange(0, dim, 2, dtype=jnp.float32) / dim)
    ang = positions.astype(jnp.float32)[..., None] * inv_freq
    return jnp.cos(ang), jnp.sin(ang)


def apply_rope(t, cos, sin):
    tf = t.astype(jnp.float32)
    t1, t2 = jnp.split(tf, 2, axis=-1)
    return jnp.concatenate([t1 * cos - t2 * sin, t2 * cos + t1 * sin], axis=-1).astype(t.dtype)


def retention_one_direction(q, k, v, log_gamma, inclusive):
    C = q.shape[3]
    idx = jnp.arange(C, dtype=jnp.float32)
    diff = idx[:, None] - idx[None, :]
    mask = (diff >= 0) if inclusive else (diff > 0)
    lg = log_gamma[:, None, None]
    intra_decay = jnp.where(mask, jnp.exp(jnp.where(mask, diff, 0.0) * lg), 0.0)
    scores = jnp.einsum('bhncd,bhnsd->bhncs', q, k) * intra_decay[None, :, None]
    intra = jnp.einsum('bhncs,bhnse->bhnce', scores, v)
    zeta = jnp.exp((C - 1 - idx)[None, :] * log_gamma[:, None])
    xi = jnp.exp((idx + 1)[None, :] * log_gamma[:, None])
    chunk_kv = jnp.einsum('bhnsd,bhnse->bhnde', k * zeta[None, :, None, :, None], v)
    chunk_decay = jnp.exp(C * log_gamma)[None, :, None, None]

    def step(state, kv):
        return state * chunk_decay + kv, state

    _, prev = lax.scan(step, jnp.zeros_like(chunk_kv[:, :, 0]), jnp.moveaxis(chunk_kv, 2, 0))
    prev = jnp.moveaxis(prev, 0, 2)
    cross = jnp.einsum('bhncd,bhnde->bhnce', q * xi[None, :, None, :, None], prev)
    return intra + cross


def bidirectional_retention(q, k, v, log_gamma_fb):
    B, S, H, d = q.shape
    nc = S // RET_CHUNK

    def to_chunks(t):
        return t.astype(jnp.float32).reshape(B, nc, RET_CHUNK, H, d).transpose(0, 3, 1, 2, 4)

    def from_chunks(t):
        return t.transpose(0, 2, 3, 1, 4).reshape(B, S, H, d)

    fwd = retention_one_direction(to_chunks(q), to_chunks(k), to_chunks(v), log_gamma_fb[0], True)
    qb, kb, vb = (jnp.flip(t, axis=1) for t in (q, k, v))
    bwd = retention_one_direction(to_chunks(qb), to_chunks(kb), to_chunks(vb), log_gamma_fb[1], False)
    return from_chunks(fwd) + jnp.flip(from_chunks(bwd), axis=1)


def mla_attention(q_nope, q_rope, k_nope, k_rope, v):
    B, S, H, _ = q_nope.shape
    nq = S // Q_BLOCK
    scale = MLA_QK_DIM ** -0.5

    def block(args):
        qn, qr = args
        s = (jnp.einsum('bqhd,bkhd->bhqk', qn, k_nope)
             + jnp.einsum('bqhr,bkr->bhqk', qr, k_rope)).astype(jnp.float32) * scale
        p = jax.nn.softmax(s, axis=-1).astype(v.dtype)
        return jnp.einsum('bhqk,bkhd->bqhd', p, v)

    def to_blocks(t):
        return jnp.moveaxis(t.reshape(B, nq, Q_BLOCK, *t.shape[2:]), 1, 0)

    out = lax.map(block, (to_blocks(q_nope), to_blocks(q_rope)))
    return jnp.moveaxis(out, 0, 1).reshape(B, S, H * MLA_V_DIM)


def setup_inputs(seed: int = 0) -> dict:
    key = jax.random.key(seed)
    ks = jax.random.split(key, 24)
    f32 = jnp.float32

    def w(k, shape, fan_in, gain=1.0):
        return jax.random.normal(k, shape, f32) * (gain * fan_in ** -0.5)

    def gain(k, shape):
        return 1.0 + 0.02 * jax.random.normal(k, shape, f32)

    x = jax.random.normal(ks[0], (BATCH, SEQ, D_MODEL), f32)
    c = jax.random.normal(ks[1], (BATCH, D_MODEL), f32)
    positions = (jnp.arange(SEQ, dtype=jnp.int32)[None, :]
                 + jax.random.randint(ks[2], (BATCH, 1), 0, 1024, dtype=jnp.int32))
    gamma_ms = 1.0 - 2.0 ** (-5.0 - np.arange(RET_HEADS, dtype=np.float32))
    a0 = jnp.asarray(np.log(-np.log(gamma_ms)), f32)
    ret_decay = a0[None, None, :] + 0.05 * jax.random.normal(ks[3], (DEPTH, 2, RET_HEADS), f32)
    return {
        "x": x,
        "c": c,
        "positions": positions,
        "ada_w": w(ks[4], (DEPTH, D_MODEL, N_MOD * D_MODEL), D_MODEL, 0.5),
        "ada_b": 0.01 * jax.random.normal(ks[5], (DEPTH, N_MOD * D_MODEL), f32),
        "norm1_w": gain(ks[6], (DEPTH, D_MODEL)),
        "w_in": w(ks[7], (DEPTH, D_MODEL, IN_COLS), D_MODEL),
        "ret_decay": ret_decay,
        "ret_gn_w": gain(ks[8], (DEPTH, RET_WIDTH)),
        "ret_gn_b": 0.01 * jax.random.normal(ks[9], (DEPTH, RET_WIDTH), f32),
        "mla_q_norm_w": gain(ks[10], (DEPTH, MLA_Q_RANK)),
        "w_uq": w(ks[11], (DEPTH, MLA_Q_RANK, MLA_HEADS * MLA_QK_DIM), MLA_Q_RANK),
        "mla_kv_norm_w": gain(ks[12], (DEPTH, MLA_KV_RANK)),
        "w_ukv": w(ks[13], (DEPTH, MLA_KV_RANK, MLA_HEADS * (MLA_NOPE_DIM + MLA_V_DIM)), MLA_KV_RANK),
        "mla_out_w": gain(ks[14], (DEPTH, MLA_WIDTH)),
        "w_o": w(ks[15], (DEPTH, MIX_WIDTH, D_MODEL), MIX_WIDTH),
        "norm2_w": gain(ks[16], (DEPTH, D_MODEL)),
        "w_gate": w(ks[17], (DEPTH, D_MODEL, D_FF), D_MODEL),
        "w_up": w(ks[18], (DEPTH, D_MODEL, D_FF), D_MODEL),
        "w_down": w(ks[19], (DEPTH, D_FF, D_MODEL), D_FF),
        "final_norm_w": gain(ks[20], (D_MODEL,)),
    }


def reference(x, c, positions, ada_w, ada_b, norm1_w, w_in, ret_decay, ret_gn_w, ret_gn_b,
              mla_q_norm_w, w_uq, mla_kv_norm_w, w_ukv, mla_out_w, w_o, norm2_w,
              w_gate, w_up, w_down, final_norm_w):
    B, S, _ = x.shape
    cos_r, sin_r = rope_tables(positions, RET_HEAD_DIM)
    cos_m, sin_m = rope_tables(positions, MLA_ROPE_DIM)
    c_act = jax.nn.silu(c)
    split_at = list(np.cumsum([RET_WIDTH, RET_WIDTH, RET_WIDTH, RET_WIDTH, MLA_Q_RANK, MLA_KV_RANK]))

    for l in range(DEPTH):
        mod = (c_act @ ada_w[l] + ada_b[l])[:, None, :]
        shift_a, scale_a, gate_a, shift_f, scale_f, gate_f = jnp.split(mod, N_MOD, axis=-1)

        h = rmsnorm(x, norm1_w[l]) * (1.0 + scale_a) + shift_a
        proj = h @ w_in[l]
        q_r, k_r, v_r, g_r, cq, ckv, k_rope = jnp.split(proj, split_at, axis=-1)

        hs = (B, S, RET_HEADS, RET_HEAD_DIM)
        q_r = apply_rope(q_r.reshape(hs), cos_r[:, :, None], sin_r[:, :, None])
        k_r = apply_rope(k_r.reshape(hs), cos_r[:, :, None], sin_r[:, :, None]) * (RET_HEAD_DIM ** -0.5)
        log_gamma = -jnp.exp(ret_decay[l].astype(jnp.float32))
        y_r = bidirectional_retention(q_r, k_r, v_r.reshape(hs), log_gamma)
        mu = jnp.mean(y_r, axis=-1, keepdims=True)
        var = jnp.mean(jnp.square(y_r - mu), axis=-1, keepdims=True)
        y_r = ((y_r - mu) * lax.rsqrt(var + EPS)).reshape(B, S, RET_WIDTH)
        y_r = y_r * ret_gn_w[l].astype(jnp.float32) + ret_gn_b[l].astype(jnp.float32)
        ret_out = (jax.nn.silu(g_r.astype(jnp.float32)) * y_r).astype(x.dtype)

        q_m = (rmsnorm(cq, mla_q_norm_w[l]) @ w_uq[l]).reshape(B, S, MLA_HEADS, MLA_QK_DIM)
        q_nope, q_rope = jnp.split(q_m, [MLA_NOPE_DIM], axis=-1)
        q_rope = apply_rope(q_rope, cos_m[:, :, None], sin_m[:, :, None])
        kv = (rmsnorm(ckv, mla_kv_norm_w[l]) @ w_ukv[l]).reshape(B, S, MLA_HEADS, MLA_NOPE_DIM + MLA_V_DIM)
        k_nope, v_m = jnp.split(kv, [MLA_NOPE_DIM], axis=-1)
        k_rope = apply_rope(k_rope, cos_m, sin_m)
        mla_out = rmsnorm(mla_attention(q_nope, q_rope, k_nope, k_rope, v_m), mla_out_w[l])

        mixed = jnp.concatenate([ret_out, mla_out], axis=-1) @ w_o[l]
        x = x + gate_a * mixed

        h = rmsnorm(x, norm2_w[l]) * (1.0 + scale_f) + shift_f
        ff = (jax.nn.silu(h @ w_gate[l]) * (h @ w_up[l])) @ w_down[l]
        x = x + gate_f * ff

    return rmsnorm(x, final_norm_w)
```

```python
import functools

import jax
import jax.numpy as jnp
from jax import lax
from jax.experimental import pallas as pl
from jax.experimental.pallas import tpu as pltpu

F32 = jnp.float32
BF16 = jnp.bfloat16

EPS = 1e-6
ROPE_BASE = 10000.0
N_MOD = 6
RET_HEADS = 8
RET_HEAD_DIM = 128
RET_CHUNK = 128
MLA_HEADS = 8
MLA_NOPE_DIM = 128
MLA_ROPE_DIM = 64
MLA_V_DIM = 128
MLA_QK_DIM = MLA_NOPE_DIM + MLA_ROPE_DIM
MLA_Q_RANK = 512
MLA_KV_RANK = 512
LANES = 128
MLA_QK_PAD = 2 * LANES
MIB = 1024 * 1024
LOG2E = 1.4426950408889634

_NT = (((1,), (1,)), ((), ()))


def _cparams(sem, vmem_mib):
    return pltpu.CompilerParams(dimension_semantics=sem, vmem_limit_bytes=int(vmem_mib * MIB))


def _rms(x, w):
    return x * lax.rsqrt(jnp.mean(x * x, axis=-1, keepdims=True) + EPS) * w


def _silu(x):
    return x * jax.nn.sigmoid(x)


def _adaln_kernel(c_ref, w_ref, b_ref, o_ref):
    ca = _silu(c_ref[...]).astype(BF16)
    o_ref[...] = jnp.dot(ca, w_ref[...].astype(BF16), preferred_element_type=F32) + b_ref[...]


def _adaln(c, w, b):
    bsz, d = c.shape
    n = w.shape[1]
    tn = 1024
    return pl.pallas_call(
        _adaln_kernel,
        grid=(n // tn,),
        in_specs=[pl.BlockSpec((bsz, d), lambda j: (0, 0)),
                  pl.BlockSpec((d, tn), lambda j: (0, j)),
                  pl.BlockSpec((1, tn), lambda j: (0, j))],
        out_specs=pl.BlockSpec((bsz, tn), lambda j: (0, j)),
        out_shape=jax.ShapeDtypeStruct((bsz, n), F32),
        compiler_params=_cparams(("parallel",), 40),
        name="adaln",
    )(c, w, b.reshape(1, n))


def _rope_table_kernel(pos_ref, frq_ref, cr_ref, sr_ref, cm_ref, sa_ref, sb_ref):
    ang = pos_ref[...].astype(F32) * frq_ref[...]
    cs = jnp.cos(ang)
    sn = jnp.sin(ang)
    lane = lax.broadcasted_iota(jnp.int32, ang.shape, 1)
    cs64 = pltpu.roll(cs, 64, 1)
    sn64 = pltpu.roll(sn, 64, 1)
    cs96 = pltpu.roll(cs, 96, 1)
    sn96 = pltpu.roll(sn, 96, 1)
    cr_ref[...] = jnp.where(lane < 64, cs, cs64)
    sr_ref[...] = jnp.where(lane < 64, -sn, sn64)
    cm_ref[...] = jnp.where(lane < 32, cs64, cs96)
    sa_ref[...] = jnp.where(lane < 32, -sn64, 0.0)
    sb_ref[...] = jnp.where((lane >= 32) & (lane < 64), sn96, 0.0)


def _rope_tables(pos_col, frq):
    t = pos_col.shape[0]
    tr = min(1024, t)
    spec = pl.BlockSpec((tr, LANES), lambda i: (i, 0))
    return pl.pallas_call(
        _rope_table_kernel,
        grid=(t // tr,),
        in_specs=[pl.BlockSpec((tr, 1), lambda i: (i, 0)),
                  pl.BlockSpec((1, LANES), lambda i: (0, 0))],
        out_specs=[spec] * 5,
        out_shape=[jax.ShapeDtypeStruct((t, LANES), F32)] * 5,
        compiler_params=_cparams(("parallel",), 32),
        name="rope_tables",
    )(pos_col, frq)


def _rope128(t, c, s):
    return t * c + pltpu.roll(t, 64, 1) * s


def _rope64(t, cm, sa, sb):
    return t * cm + pltpu.roll(t, 96, 1) * sa + pltpu.roll(t, 32, 1) * sb


def _inproj_kernel(x_ref, mod_ref, n1_ref, w_ref, wr_ref, cr_ref, sr_ref, cm_ref, sa_ref, sb_ref,
                   qnw_ref, kvnw_ref, proj_ref, kr_ref, h_ref, *, k_scale):
    j = pl.program_id(1)
    nheads = proj_ref.shape[1] // LANES

    @pl.when(j == 0)
    def _():
        h = _rms(x_ref[...], n1_ref[...]) * (1.0 + mod_ref[1:2, :]) + mod_ref[0:1, :]
        h_ref[...] = h.astype(BF16)

    def tile():
        return jnp.dot(h_ref[...], w_ref[...], preferred_element_type=F32)

    def rope_store(scale):
        acc = tile()
        c = cr_ref[...]
        s = sr_ref[...]
        for hh in range(nheads):
            sl = slice(hh * LANES, (hh + 1) * LANES)
            o = _rope128(acc[:, sl], c, s)
            if scale is not None:
                o = o * scale
            proj_ref[:, sl] = o.astype(BF16)

    @pl.when(j == 0)
    def _():
        rope_store(None)

    @pl.when(j == 1)
    def _():
        rope_store(k_scale)

    @pl.when((j == 2) | (j == 3))
    def _():
        proj_ref[...] = tile().astype(BF16)

    @pl.when(j == 4)
    def _():
        acc = tile()
        r = MLA_Q_RANK
        proj_ref[:, :r] = _rms(acc[:, :r], qnw_ref[...]).astype(BF16)
        proj_ref[:, r:] = _rms(acc[:, r:], kvnw_ref[...]).astype(BF16)
        t = jnp.dot(h_ref[...], wr_ref[...], preferred_element_type=F32)
        kr_ref[...] = _rope64(t, cm_ref[...], sa_ref[...], sb_ref[...]).astype(BF16)


def _inproj(x2d, mod, n1w, w_main, w_rope, tabs, qnw, kvnw, seq):
    t, d = x2d.shape
    n = w_main.shape[1]
    tn = 1024
    assert n == 5 * tn and MLA_Q_RANK + MLA_KV_RANK == tn
    tm = min(1024, seq)
    per_b = seq // tm
    tab_spec = pl.BlockSpec((tm, LANES), lambda i, j: (i, 0))
    row = lambda width: pl.BlockSpec((1, width), lambda i, j: (0, 0))
    return pl.pallas_call(
        functools.partial(_inproj_kernel, k_scale=RET_HEAD_DIM ** -0.5),
        grid=(t // tm, n // tn),
        in_specs=[pl.BlockSpec((tm, d), lambda i, j: (i, 0)),
                  pl.BlockSpec((None, N_MOD, d), lambda i, j: (i // per_b, 0, 0)),
                  row(d),
                  pl.BlockSpec((d, tn), lambda i, j: (0, j)),
                  pl.BlockSpec((d, LANES), lambda i, j: (0, 0)),
                  tab_spec, tab_spec, tab_spec, tab_spec, tab_spec,
                  row(MLA_Q_RANK), row(MLA_KV_RANK)],
        out_specs=[pl.BlockSpec((tm, tn), lambda i, j: (i, j)),
                   pl.BlockSpec((tm, LANES), lambda i, j: (i, 0))],
        out_shape=[jax.ShapeDtypeStruct((t, n), BF16),
                   jax.ShapeDtypeStruct((t, LANES), BF16)],
        scratch_shapes=[pltpu.VMEM((tm, d), BF16)],
        compiler_params=_cparams(("parallel", "arbitrary"), 56),
        name="inproj",
    )(x2d, mod, n1w, w_main, w_rope, *tabs, qnw, kvnw)


def _mla_up_kernel(cq_ref, ckv_ref, wq_ref, wkv_ref, cm_ref, sa_ref, sb_ref, q_ref, kv_ref, *, q_scale):
    q = jnp.dot(cq_ref[...], wq_ref[...], preferred_element_type=F32)
    cm = cm_ref[...]
    sa = sa_ref[...]
    sb = sb_ref[...]
    for hh in range(MLA_HEADS):
        base = hh * MLA_QK_PAD
        q_ref[:, base:base + LANES] = (q[:, base:base + LANES] * q_scale).astype(BF16)
        t = q[:, base + LANES:base + 2 * LANES]
        q_ref[:, base + LANES:base + 2 * LANES] = (_rope64(t, cm, sa, sb) * q_scale).astype(BF16)
    kv_ref[...] = jnp.dot(ckv_ref[...], wkv_ref[...], preferred_element_type=F32).astype(BF16)


def _mla_up(proj, w_uq_p, w_ukv_p, tabs_m, seq):
    t = proj.shape[0]
    tm = min(1024, seq)
    r = MLA_Q_RANK
    cq_blk = (4 * RET_HEADS * RET_HEAD_DIM) // r
    nq = w_uq_p.shape[1]
    nkv = w_ukv_p.shape[1]
    tab_spec = pl.BlockSpec((tm, LANES), lambda i: (i, 0))
    return pl.pallas_call(
        functools.partial(_mla_up_kernel, q_scale=MLA_QK_DIM ** -0.5 * LOG2E),
        grid=(t // tm,),
        in_specs=[pl.BlockSpec((tm, r), lambda i: (i, cq_blk)),
                  pl.BlockSpec((tm, r), lambda i: (i, cq_blk + 1)),
                  pl.BlockSpec((r, nq), lambda i: (0, 0)),
                  pl.BlockSpec((r, nkv), lambda i: (0, 0)),
                  tab_spec, tab_spec, tab_spec],
        out_specs=[pl.BlockSpec((tm, nq), lambda i: (i, 0)),
                   pl.BlockSpec((tm, nkv), lambda i: (i, 0))],
        out_shape=[jax.ShapeDtypeStruct((t, nq), BF16),
                   jax.ShapeDtypeStruct((t, nkv), BF16)],
        compiler_params=_cparams(("parallel",), 48),
        name="mla_up",
    )(proj, proj, w_uq_p, w_ukv_p, *tabs_m)


def _attn_kernel(q_ref, kn_ref, kr_ref, v_ref, o_ref, kcat_ref):
    @pl.when(pl.program_id(2) == 0)
    def _():
        kcat_ref[:, :LANES] = kn_ref[...]
        kcat_ref[:, LANES:] = kr_ref[...]

    s = lax.dot_general(q_ref[...], kcat_ref[...], _NT, preferred_element_type=F32)
    m = jnp.max(s, axis=-1, keepdims=True)
    p = jnp.exp2(s - m)
    l = jnp.sum(p, axis=-1, keepdims=True)
    o = jnp.dot(p.astype(BF16), v_ref[...], preferred_element_type=F32)
    o_ref[...] = (o / l).astype(o_ref.dtype)


def _attention(q, kv, kr, bsz, seq):
    t = q.shape[0]
    tq = min(256, seq)
    nq = seq // tq
    return pl.pallas_call(
        _attn_kernel,
        grid=(bsz, MLA_HEADS, nq),
        in_specs=[pl.BlockSpec((tq, MLA_QK_PAD), lambda b, h, i: (b * nq + i, h)),
                  pl.BlockSpec((seq, LANES), lambda b, h, i: (b, h)),
                  pl.BlockSpec((seq, LANES), lambda b, h, i: (b, 0)),
                  pl.BlockSpec((seq, LANES), lambda b, h, i: (b, MLA_HEADS + h))],
        out_specs=pl.BlockSpec((tq, MLA_V_DIM), lambda b, h, i: (b * nq + i, h)),
        out_shape=jax.ShapeDtypeStruct((t, MLA_HEADS * MLA_V_DIM), BF16),
        scratch_shapes=[pltpu.VMEM((seq, MLA_QK_PAD), BF16)],
        compiler_params=_cparams(("parallel", "parallel", "arbitrary"), 48),
        name="mla_attn",
    )(q, kv, kr, kv)


def _ret_kernel(lg_ref, q_ref, k_ref, v_ref, g_ref, gw_ref, gb_ref, o_ref, y_ref):
    h = pl.program_id(1)
    lgf = lg_ref[0, h]
    lgb = lg_ref[1, h]
    c = RET_CHUNK
    nc = q_ref.shape[0] // c
    ii = lax.broadcasted_iota(jnp.int32, (c, c), 0).astype(F32)
    jj = lax.broadcasted_iota(jnp.int32, (c, c), 1).astype(F32)
    diff = ii - jj
    dmat = jnp.where(diff >= 0, jnp.exp(jnp.maximum(diff, 0.0) * lgf), jnp.exp(jnp.maximum(-diff, 0.0) * lgb))
    xi_f = jnp.exp((ii + 1.0) * lgf)
    zeta_f = jnp.exp((c - 1.0 - ii) * lgf)
    xi_b = jnp.exp((c - ii) * lgb)
    zeta_b = jnp.exp(ii * lgb)
    dec_f = jnp.exp(jnp.full((c, c), c, F32) * lgf)
    dec_b = jnp.exp(jnp.full((c, c), c, F32) * lgb)

    def state_update(state, k, v, zeta, dec):
        kz = (k.astype(F32) * zeta).T.astype(BF16)
        return state * dec + jnp.dot(kz, v, preferred_element_type=F32)

    def cross(q, xi, state):
        qx = (q.astype(F32) * xi).astype(BF16)
        return jnp.dot(qx, state.astype(BF16), preferred_element_type=F32)

    def fwd(n, state):
        rows = pl.ds(pl.multiple_of(n * c, c), c)
        q = q_ref[rows, :]
        k = k_ref[rows, :]
        v = v_ref[rows, :]
        s = lax.dot_general(q, k, _NT, preferred_element_type=F32) * dmat
        intra = jnp.dot(s.astype(BF16), v, preferred_element_type=F32)
        y_ref[rows, :] = intra + cross(q, xi_f, state)
        return state_update(state, k, v, zeta_f, dec_f)

    lax.fori_loop(0, nc, fwd, jnp.zeros((c, c), F32))

    def bwd(step, state):
        n = nc - 1 - step
        rows = pl.ds(pl.multiple_of(n * c, c), c)
        q = q_ref[rows, :]
        y = y_ref[rows, :] + cross(q, xi_b, state)
        mu = jnp.mean(y, axis=-1, keepdims=True)
        yc = y - mu
        var = jnp.mean(yc * yc, axis=-1, keepdims=True)
        yn = yc * lax.rsqrt(var + EPS) * gw_ref[...] + gb_ref[...]
        o_ref[rows, :] = (_silu(g_ref[rows, :].astype(F32)) * yn).astype(o_ref.dtype)
        return state_update(state, k_ref[rows, :], v_ref[rows, :], zeta_b, dec_b)

    lax.fori_loop(0, nc, bwd, jnp.zeros((c, c), F32))


def _retention(log_gamma, proj, gn_w, gn_b, bsz, seq):
    t = proj.shape[0]
    hd = RET_HEAD_DIM
    nh = RET_HEADS
    blk = lambda off: pl.BlockSpec((seq, hd), lambda b, h: (b, off + h))
    row = pl.BlockSpec((1, hd), lambda b, h: (0, h))
    return pl.pallas_call(
        _ret_kernel,
        grid=(bsz, nh),
        in_specs=[pl.BlockSpec(memory_space=pltpu.SMEM),
                  blk(0), blk(nh), blk(2 * nh), blk(3 * nh), row, row],
        out_specs=pl.BlockSpec((seq, hd), lambda b, h: (b, h)),
        out_shape=jax.ShapeDtypeStruct((t, nh * hd), BF16),
        scratch_shapes=[pltpu.VMEM((seq, hd), F32)],
        compiler_params=_cparams(("parallel", "parallel"), 32),
        name="retention",
    )(log_gamma, proj, proj, proj, proj, gn_w, gn_b)


def _oproj_kernel(r_ref, a_ref, x_ref, mod_ref, ow_ref, wo_ref, o_ref):
    kr = r_ref.shape[1]
    an = _rms(a_ref[...].astype(F32), ow_ref[...]).astype(BF16)
    mixed = jnp.dot(r_ref[...], wo_ref[:kr, :], preferred_element_type=F32)
    mixed = mixed + jnp.dot(an, wo_ref[kr:, :], preferred_element_type=F32)
    o_ref[...] = x_ref[...] + mod_ref[2:3, :] * mixed


def _oproj(ret_out, attn, x2d, mod, ow, w_o, seq):
    t, d = x2d.shape
    tm = min(512, seq)
    per_b = seq // tm
    kr = ret_out.shape[1]
    ka = attn.shape[1]
    return pl.pallas_call(
        _oproj_kernel,
        grid=(t // tm,),
        in_specs=[pl.BlockSpec((tm, kr), lambda i: (i, 0)),
                  pl.BlockSpec((tm, ka), lambda i: (i, 0)),
                  pl.BlockSpec((tm, d), lambda i: (i, 0)),
                  pl.BlockSpec((None, N_MOD, d), lambda i: (i // per_b, 0, 0)),
                  pl.BlockSpec((1, ka), lambda i: (0, 0)),
                  pl.BlockSpec((kr + ka, d), lambda i: (0, 0))],
        out_specs=pl.BlockSpec((tm, d), lambda i: (i, 0)),
        out_shape=jax.ShapeDtypeStruct((t, d), F32),
        compiler_params=_cparams(("parallel",), 48),
        name="oproj",
    )(ret_out, attn, x2d, mod, ow, w_o)


def _ffn_kernel(x_ref, mod_ref, n2_ref, wg_ref, wu_ref, wd_ref, fw_ref, o_ref, h_ref, acc_ref):
    f = pl.program_id(1)

    @pl.when(f == 0)
    def _():
        h = _rms(x_ref[...], n2_ref[...]) * (1.0 + mod_ref[4:5, :]) + mod_ref[3:4, :]
        h_ref[...] = h.astype(BF16)

    h = h_ref[...]
    g = jnp.dot(h, wg_ref[...], preferred_element_type=F32)
    u = jnp.dot(h, wu_ref[...], preferred_element_type=F32)
    a = (_silu(g) * u).astype(BF16)
    d = jnp.dot(a, wd_ref[...], preferred_element_type=F32)

    @pl.when(f == 0)
    def _():
        acc_ref[...] = d

    @pl.when(f > 0)
    def _():
        acc_ref[...] += d

    @pl.when(f == pl.num_programs(1) - 1)
    def _():
        x2 = x_ref[...] + mod_ref[5:6, :] * acc_ref[...]
        o_ref[...] = _rms(x2, fw_ref[...])


def _ffn(x2d, mod, n2w, wg, wu, wd, fw, seq):
    t, d = x2d.shape
    dff = wg.shape[1]
    tm = min(512, seq)
    tf = 512
    per_b = seq // tm
    row = pl.BlockSpec((1, d), lambda i, f: (0, 0))
    return pl.pallas_call(
        _ffn_kernel,
        grid=(t // tm, dff // tf),
        in_specs=[pl.BlockSpec((tm, d), lambda i, f: (i, 0)),
                  pl.BlockSpec((None, N_MOD, d), lambda i, f: (i // per_b, 0, 0)),
                  row,
                  pl.BlockSpec((d, tf), lambda i, f: (0, f)),
                  pl.BlockSpec((d, tf), lambda i, f: (0, f)),
                  pl.BlockSpec((tf, d), lambda i, f: (f, 0)),
                  row],
        out_specs=pl.BlockSpec((tm, d), lambda i, f: (i, 0)),
        out_shape=jax.ShapeDtypeStruct((t, d), F32),
        scratch_shapes=[pltpu.VMEM((tm, d), BF16), pltpu.VMEM((tm, d), F32)],
        compiler_params=_cparams(("parallel", "arbitrary"), 52),
        name="ffn",
    )(x2d, mod, n2w, wg, wu, wd, fw)


def _rope_freq_row():
    fr = ROPE_BASE ** (-jnp.arange(0, RET_HEAD_DIM, 2, dtype=F32) / RET_HEAD_DIM)
    fm = ROPE_BASE ** (-jnp.arange(0, MLA_ROPE_DIM, 2, dtype=F32) / MLA_ROPE_DIM)
    pad = jnp.zeros((LANES - fr.shape[0] - fm.shape[0],), F32)
    return jnp.concatenate([fr, fm, pad]).reshape(1, LANES)


def kernel(x, c, positions, ada_w, ada_b, norm1_w, w_in, ret_decay, ret_gn_w, ret_gn_b, mla_q_norm_w, w_uq,
           mla_kv_norm_w, w_ukv, mla_out_w, w_o, norm2_w, w_gate, w_up, w_down, final_norm_w):
    bsz, seq, d = x.shape
    t = bsz * seq
    assert ada_w.shape[0] == 1, "the final norm is fused into the (single) layer's FFN kernel"
    ret_w = RET_HEADS * RET_HEAD_DIM
    n_main = 4 * ret_w + MLA_Q_RANK + MLA_KV_RANK

    tabs = _rope_tables(positions.reshape(t, 1), _rope_freq_row())
    xc = x.reshape(t, d)
    mod = _adaln(c, ada_w[0], ada_b[0]).reshape(bsz, N_MOD, d)

    w_main = w_in[0][:, :n_main].astype(BF16)
    w_rope = jnp.pad(w_in[0][:, n_main:], ((0, 0), (0, LANES - MLA_ROPE_DIM))).astype(BF16)
    proj, k_rope = _inproj(xc, mod, norm1_w[0].reshape(1, d), w_main, w_rope, tabs,
                           mla_q_norm_w[0].reshape(1, -1), mla_kv_norm_w[0].reshape(1, -1), seq)

    wq = w_uq[0].reshape(MLA_Q_RANK, MLA_HEADS, MLA_QK_DIM)
    wq = jnp.pad(wq, ((0, 0), (0, 0), (0, MLA_QK_PAD - MLA_QK_DIM)))
    w_uq_p = wq.reshape(MLA_Q_RANK, MLA_HEADS * MLA_QK_PAD).astype(BF16)
    wkv = w_ukv[0].reshape(MLA_KV_RANK, MLA_HEADS, MLA_NOPE_DIM + MLA_V_DIM)
    w_ukv_p = jnp.concatenate(
        [wkv[:, :, :MLA_NOPE_DIM].reshape(MLA_KV_RANK, -1), wkv[:, :, MLA_NOPE_DIM:].reshape(MLA_KV_RANK, -1)],
        axis=1).astype(BF16)
    q_m, kv_m = _mla_up(proj, w_uq_p, w_ukv_p, tabs[2:], seq)
    attn = _attention(q_m, kv_m, k_rope, bsz, seq)

    log_gamma = -jnp.exp(ret_decay[0].astype(F32))
    ret_out = _retention(log_gamma, proj, ret_gn_w[0].reshape(1, -1), ret_gn_b[0].reshape(1, -1), bsz, seq)

    x1 = _oproj(ret_out, attn, xc, mod, mla_out_w[0].reshape(1, -1), w_o[0].astype(BF16), seq)
    out = _ffn(x1, mod, norm2_w[0].reshape(1, d), w_gate[0].astype(BF16), w_up[0].astype(BF16),
               w_down[0].astype(BF16), final_norm_w.reshape(1, d), seq)
    return out.reshape(bsz, seq, d)
```

```python
import functools

import jax
import jax.numpy as jnp
from jax import lax
from jax.experimental import pallas as pl
from jax.experimental.pallas import tpu as pltpu

F32 = jnp.float32
BF16 = jnp.bfloat16

EPS = 1e-6
ROPE_BASE = 10000.0
N_MOD = 6
RET_HEADS = 8
RET_HEAD_DIM = 128
RET_CHUNK = 128
RET_UNROLL = 8
MLA_HEADS = 8
MLA_NOPE_DIM = 128
MLA_ROPE_DIM = 64
MLA_V_DIM = 128
MLA_QK_DIM = MLA_NOPE_DIM + MLA_ROPE_DIM
MLA_Q_RANK = 512
MLA_KV_RANK = 512
LANES = 128
MLA_QK_PAD = 2 * LANES
MIB = 1024 * 1024
LOG2E = 1.4426950408889634

_NT = (((1,), (1,)), ((), ()))


def _cparams(sem, vmem_mib):
    return pltpu.CompilerParams(dimension_semantics=sem, vmem_limit_bytes=int(vmem_mib * MIB))


def _rms(x, w):
    return x * lax.rsqrt(jnp.mean(x * x, axis=-1, keepdims=True) + EPS) * w


def _silu(x):
    return x * jax.nn.sigmoid(x)


def _adaln_kernel(c_ref, w_ref, b_ref, o_ref):
    ca = _silu(c_ref[...]).astype(BF16)
    o_ref[...] = jnp.dot(ca, w_ref[...].astype(BF16), preferred_element_type=F32) + b_ref[...]


def _adaln(c, w, b):
    bsz, d = c.shape
    n = w.shape[1]
    tn = 1024
    return pl.pallas_call(
        _adaln_kernel,
        grid=(n // tn,),
        in_specs=[pl.BlockSpec((bsz, d), lambda j: (0, 0)),
                  pl.BlockSpec((d, tn), lambda j: (0, j)),
                  pl.BlockSpec((1, tn), lambda j: (0, j))],
        out_specs=pl.BlockSpec((bsz, tn), lambda j: (0, j)),
        out_shape=jax.ShapeDtypeStruct((bsz, n), F32),
        compiler_params=_cparams(("parallel",), 40),
        name="adaln",
    )(c, w, b.reshape(1, n))


def _rope_table_kernel(pos_ref, frq_ref, cr_ref, sr_ref, cm_ref, sa_ref, sb_ref):
    ang = pos_ref[...].astype(F32) * frq_ref[...]
    cs = jnp.cos(ang)
    sn = jnp.sin(ang)
    lane = lax.broadcasted_iota(jnp.int32, ang.shape, 1)
    cs64 = pltpu.roll(cs, 64, 1)
    sn64 = pltpu.roll(sn, 64, 1)
    cs96 = pltpu.roll(cs, 96, 1)
    sn96 = pltpu.roll(sn, 96, 1)
    cr_ref[...] = jnp.where(lane < 64, cs, cs64)
    sr_ref[...] = jnp.where(lane < 64, -sn, sn64)
    cm_ref[...] = jnp.where(lane < 32, cs64, cs96)
    sa_ref[...] = jnp.where(lane < 32, -sn64, 0.0)
    sb_ref[...] = jnp.where((lane >= 32) & (lane < 64), sn96, 0.0)


def _rope_tables(pos_col, frq):
    t = pos_col.shape[0]
    tr = min(1024, t)
    spec = pl.BlockSpec((tr, LANES), lambda i: (i, 0))
    return pl.pallas_call(
        _rope_table_kernel,
        grid=(t // tr,),
        in_specs=[pl.BlockSpec((tr, 1), lambda i: (i, 0)),
                  pl.BlockSpec((1, LANES), lambda i: (0, 0))],
        out_specs=[spec] * 5,
        out_shape=[jax.ShapeDtypeStruct((t, LANES), F32)] * 5,
        compiler_params=_cparams(("parallel",), 32),
        name="rope_tables",
    )(pos_col, frq)


def _rope128(t, c, s):
    return t * c + pltpu.roll(t, 64, 1) * s


def _rope64(t, cm, sa, sb):
    return t * cm + pltpu.roll(t, 96, 1) * sa + pltpu.roll(t, 32, 1) * sb


def _inproj_kernel(x_ref, mod_ref, n1_ref, w_ref, wr_ref, cr_ref, sr_ref, cm_ref, sa_ref, sb_ref,
                   qnw_ref, kvnw_ref, proj_ref, kr_ref, h_ref, *, k_scale):
    j = pl.program_id(1)
    nheads = proj_ref.shape[1] // LANES

    @pl.when(j == 0)
    def _():
        h = _rms(x_ref[...], n1_ref[...]) * (1.0 + mod_ref[1:2, :]) + mod_ref[0:1, :]
        h_ref[...] = h.astype(BF16)

    def tile():
        return jnp.dot(h_ref[...], w_ref[...], preferred_element_type=F32)

    def rope_store(scale):
        acc = tile()
        c = cr_ref[...]
        s = sr_ref[...]
        for hh in range(nheads):
            sl = slice(hh * LANES, (hh + 1) * LANES)
            o = _rope128(acc[:, sl], c, s)
            if scale is not None:
                o = o * scale
            proj_ref[:, sl] = o.astype(BF16)

    @pl.when(j == 0)
    def _():
        rope_store(None)

    @pl.when(j == 1)
    def _():
        rope_store(k_scale)

    @pl.when((j == 2) | (j == 3))
    def _():
        proj_ref[...] = tile().astype(BF16)

    @pl.when(j == 4)
    def _():
        acc = tile()
        r = MLA_Q_RANK
        proj_ref[:, :r] = _rms(acc[:, :r], qnw_ref[...]).astype(BF16)
        proj_ref[:, r:] = _rms(acc[:, r:], kvnw_ref[...]).astype(BF16)
        t = jnp.dot(h_ref[...], wr_ref[...], preferred_element_type=F32)
        kr_ref[...] = _rope64(t, cm_ref[...], sa_ref[...], sb_ref[...]).astype(BF16)


def _inproj(x2d, mod, n1w, w_main, w_rope, tabs, qnw, kvnw, seq):
    t, d = x2d.shape
    n = w_main.shape[1]
    tn = 1024
    assert n == 5 * tn and MLA_Q_RANK + MLA_KV_RANK == tn
    tm = min(1024, seq)
    per_b = seq // tm
    tab_spec = pl.BlockSpec((tm, LANES), lambda i, j: (i, 0))
    row = lambda width: pl.BlockSpec((1, width), lambda i, j: (0, 0))
    return pl.pallas_call(
        functools.partial(_inproj_kernel, k_scale=RET_HEAD_DIM ** -0.5),
        grid=(t // tm, n // tn),
        in_specs=[pl.BlockSpec((tm, d), lambda i, j: (i, 0)),
                  pl.BlockSpec((None, N_MOD, d), lambda i, j: (i // per_b, 0, 0)),
                  row(d),
                  pl.BlockSpec((d, tn), lambda i, j: (0, j)),
                  pl.BlockSpec((d, LANES), lambda i, j: (0, 0)),
                  tab_spec, tab_spec, tab_spec, tab_spec, tab_spec,
                  row(MLA_Q_RANK), row(MLA_KV_RANK)],
        out_specs=[pl.BlockSpec((tm, tn), lambda i, j: (i, j)),
                   pl.BlockSpec((tm, LANES), lambda i, j: (i, 0))],
        out_shape=[jax.ShapeDtypeStruct((t, n), BF16),
                   jax.ShapeDtypeStruct((t, LANES), BF16)],
        scratch_shapes=[pltpu.VMEM((tm, d), BF16)],
        compiler_params=_cparams(("parallel", "arbitrary"), 56),
        name="inproj",
    )(x2d, mod, n1w, w_main, w_rope, *tabs, qnw, kvnw)


def _mla_up_kernel(cq_ref, ckv_ref, wq_ref, wkv_ref, cm_ref, sa_ref, sb_ref, q_ref, kv_ref, *, q_scale):
    q = jnp.dot(cq_ref[...], wq_ref[...], preferred_element_type=F32)
    cm = cm_ref[...]
    sa = sa_ref[...]
    sb = sb_ref[...]
    for hh in range(MLA_HEADS):
        base = hh * MLA_QK_PAD
        q_ref[:, base:base + LANES] = (q[:, base:base + LANES] * q_scale).astype(BF16)
        t = q[:, base + LANES:base + 2 * LANES]
        q_ref[:, base + LANES:base + 2 * LANES] = (_rope64(t, cm, sa, sb) * q_scale).astype(BF16)
    kv_ref[...] = jnp.dot(ckv_ref[...], wkv_ref[...], preferred_element_type=F32).astype(BF16)


def _mla_up(proj, w_uq_p, w_ukv_p, tabs_m, seq):
    t = proj.shape[0]
    tm = min(1024, seq)
    r = MLA_Q_RANK
    cq_blk = (4 * RET_HEADS * RET_HEAD_DIM) // r
    nq = w_uq_p.shape[1]
    nkv = w_ukv_p.shape[1]
    tab_spec = pl.BlockSpec((tm, LANES), lambda i: (i, 0))
    return pl.pallas_call(
        functools.partial(_mla_up_kernel, q_scale=MLA_QK_DIM ** -0.5 * LOG2E),
        grid=(t // tm,),
        in_specs=[pl.BlockSpec((tm, r), lambda i: (i, cq_blk)),
                  pl.BlockSpec((tm, r), lambda i: (i, cq_blk + 1)),
                  pl.BlockSpec((r, nq), lambda i: (0, 0)),
                  pl.BlockSpec((r, nkv), lambda i: (0, 0)),
                  tab_spec, tab_spec, tab_spec],
        out_specs=[pl.BlockSpec((tm, nq), lambda i: (i, 0)),
                   pl.BlockSpec((tm, nkv), lambda i: (i, 0))],
        out_shape=[jax.ShapeDtypeStruct((t, nq), BF16),
                   jax.ShapeDtypeStruct((t, nkv), BF16)],
        compiler_params=_cparams(("parallel",), 48),
        name="mla_up",
    )(proj, proj, w_uq_p, w_ukv_p, *tabs_m)


def _attn_kernel(q_ref, kn_ref, kr_ref, v_ref, o_ref, kcat_ref, vt_ref, s0_ref, s1_ref, m0_ref, m1_ref, *, tq, tk):
    seq = q_ref.shape[0]
    nq = seq // tq
    nk = seq // tk
    kcat_ref[:, :LANES] = kn_ref[...]
    kcat_ref[:, LANES:] = kr_ref[...]
    vt_ref[...] = v_ref[...].astype(F32).T.astype(BF16)
    s_refs = (s0_ref, s1_ref)
    m_refs = (m0_ref, m1_ref)

    def scores_chunk(q, slot, c, m):
        ks = slice(c * tk, (c + 1) * tk)
        s = lax.dot_general(kcat_ref[ks, :], q, _NT, preferred_element_type=F32)
        mc = jnp.max(s, axis=0, keepdims=True)
        s_refs[slot][ks, :] = s
        return mc if m is None else jnp.maximum(m, mc)

    def finish_chunk(slot, c, m, l, acc):
        ks = slice(c * tk, (c + 1) * tk)
        p = jnp.exp2(s_refs[slot][ks, :] - m)
        lc = jnp.sum(p, axis=0, keepdims=True)
        ac = jnp.dot(vt_ref[:, ks], p.astype(BF16), preferred_element_type=F32)
        return (lc if l is None else l + lc), (ac if acc is None else acc + ac)

    def step(i, slot, do_scores, do_finish):
        if do_scores:
            qn = q_ref[pl.ds(pl.multiple_of((i + 1) * tq, tq), tq), :]
        if do_finish:
            mcur = m_refs[slot][...]
        m = l = acc = None
        for c in range(nk):
            if do_scores:
                m = scores_chunk(qn, 1 - slot, c, m)
            if do_finish:
                l, acc = finish_chunk(slot, c, mcur, l, acc)
        if do_scores:
            m_refs[1 - slot][...] = m
        if do_finish:
            rows = pl.ds(pl.multiple_of(i * tq, tq), tq)
            o_ref[rows, :] = (acc / l).T.astype(o_ref.dtype)

    step(-1, 1, True, False)

    def pair(j, carry):
        step(2 * j, 0, True, True)
        step(2 * j + 1, 1, True, True)
        return carry

    lax.fori_loop(0, nq // 2 - 1, pair, 0)
    step(nq - 2, 0, True, True)
    step(nq - 1, 1, False, True)


def _attention(q, kv, kr, bsz, seq):
    t = q.shape[0]
    tq = min(512, seq // 2)
    tk = min(512, seq)
    assert seq % (2 * tq) == 0 and seq % tk == 0
    kv_blk = lambda off: pl.BlockSpec((seq, LANES), lambda b, h: (b, off + h))
    return pl.pallas_call(
        functools.partial(_attn_kernel, tq=tq, tk=tk),
        grid=(bsz, MLA_HEADS),
        in_specs=[pl.BlockSpec((seq, MLA_QK_PAD), lambda b, h: (b, h)),
                  kv_blk(0),
                  pl.BlockSpec((seq, LANES), lambda b, h: (b, 0)),
                  kv_blk(MLA_HEADS)],
        out_specs=pl.BlockSpec((seq, MLA_V_DIM), lambda b, h: (b, h)),
        out_shape=jax.ShapeDtypeStruct((t, MLA_HEADS * MLA_V_DIM), BF16),
        scratch_shapes=[pltpu.VMEM((seq, MLA_QK_PAD), BF16),
                        pltpu.VMEM((MLA_V_DIM, seq), BF16),
                        pltpu.VMEM((seq, tq), F32), pltpu.VMEM((seq, tq), F32),
                        pltpu.VMEM((1, tq), F32), pltpu.VMEM((1, tq), F32)],
        compiler_params=_cparams(("parallel", "parallel"), 56),
        name="mla_attn",
    )(q, kv, kr, kv)


def _ret_kernel(lg_ref, q_ref, k_ref, v_ref, g_ref, gw_ref, gb_ref, o_ref, p_ref, kv_ref, st_ref, y_ref):
    h = pl.program_id(1)
    lgf = lg_ref[0, h]
    lgb = lg_ref[1, h]
    c = RET_CHUNK
    d = q_ref.shape[1]
    nc = q_ref.shape[0] // c
    ii = lax.broadcasted_iota(jnp.int32, (c, c), 0).astype(F32)
    jj = lax.broadcasted_iota(jnp.int32, (c, c), 1).astype(F32)
    diff = ii - jj
    dmat = jnp.where(diff >= 0, jnp.exp(jnp.maximum(diff, 0.0) * lgf), jnp.exp(jnp.maximum(-diff, 0.0) * lgb))
    qpos = lax.broadcasted_iota(jnp.int32, (c, d), 0).astype(F32)
    kpos = lax.broadcasted_iota(jnp.int32, (d, c), 1).astype(F32)
    xi_f = jnp.exp((qpos + 1.0) * lgf)
    xi_b = jnp.exp((c - qpos) * lgb)
    zeta_f = jnp.exp((c - 1.0 - kpos) * lgf)
    zeta_b = jnp.exp(kpos * lgb)
    dec_f = jnp.exp(jnp.full((d, d), c, F32) * lgf)
    dec_b = jnp.exp(jnp.full((d, d), c, F32) * lgb)

    def chunk_rows(n):
        return pl.ds(pl.multiple_of(n * c, c), c)

    def chunk_sums(n, carry):
        rows = chunk_rows(n)
        kt = k_ref[rows, :].astype(F32).T
        lhs = jnp.concatenate([(kt * zeta_f).astype(BF16), (kt * zeta_b).astype(BF16)], axis=0)
        kv_ref[n] = jnp.dot(lhs, v_ref[rows, :], preferred_element_type=F32)
        s = jnp.dot(q_ref[rows, :], kt.astype(BF16), preferred_element_type=F32) * dmat
        p_ref[rows, :] = s.astype(BF16)
        return carry

    lax.fori_loop(0, nc, chunk_sums, 0, unroll=RET_UNROLL)

    def scan(n, carry):
        sf, sb = carry
        nb = nc - 1 - n
        st_ref[n, :d, :] = sf.astype(BF16)
        st_ref[nb, d:, :] = sb.astype(BF16)
        return sf * dec_f + kv_ref[n, :d, :], sb * dec_b + kv_ref[nb, d:, :]

    zero = jnp.zeros((d, d), F32)
    lax.fori_loop(0, nc, scan, (zero, zero), unroll=RET_UNROLL)

    def outputs(n, carry):
        rows = chunk_rows(n)
        qf = q_ref[rows, :].astype(F32)
        qx = jnp.concatenate([(qf * xi_f).astype(BF16), (qf * xi_b).astype(BF16)], axis=1)
        y = jnp.dot(p_ref[rows, :], v_ref[rows, :], preferred_element_type=F32)
        y_ref[rows, :] = y + jnp.dot(qx, st_ref[n], preferred_element_type=F32)
        return carry

    lax.fori_loop(0, nc, outputs, 0, unroll=RET_UNROLL)

    def norm_gate(n, carry):
        rows = chunk_rows(n)
        y = y_ref[rows, :]
        mu = jnp.mean(y, axis=-1, keepdims=True)
        yc = y - mu
        var = jnp.mean(yc * yc, axis=-1, keepdims=True)
        yn = yc * lax.rsqrt(var + EPS) * gw_ref[...] + gb_ref[...]
        o_ref[rows, :] = (_silu(g_ref[rows, :].astype(F32)) * yn).astype(o_ref.dtype)
        return carry

    lax.fori_loop(0, nc, norm_gate, 0, unroll=RET_UNROLL)


def _retention(log_gamma, proj, gn_w, gn_b, bsz, seq):
    t = proj.shape[0]
    hd = RET_HEAD_DIM
    nh = RET_HEADS
    blk = lambda off: pl.BlockSpec((seq, hd), lambda b, h: (b, off + h))
    row = pl.BlockSpec((1, hd), lambda b, h: (0, h))
    return pl.pallas_call(
        _ret_kernel,
        grid=(bsz, nh),
        in_specs=[pl.BlockSpec(memory_space=pltpu.SMEM),
                  blk(0), blk(nh), blk(2 * nh), blk(3 * nh), row, row],
        out_specs=pl.BlockSpec((seq, hd), lambda b, h: (b, h)),
        out_shape=jax.ShapeDtypeStruct((t, nh * hd), BF16),
        scratch_shapes=[pltpu.VMEM((seq, RET_CHUNK), BF16),
                        pltpu.VMEM((seq // RET_CHUNK, 2 * hd, hd), F32),
                        pltpu.VMEM((seq // RET_CHUNK, 2 * hd, hd), BF16),
                        pltpu.VMEM((seq, hd), F32)],
        compiler_params=_cparams(("parallel", "parallel"), 32),
        name="retention",
    )(log_gamma, proj, proj, proj, proj, gn_w, gn_b)


def _oproj_kernel(r_ref, a_ref, x_ref, mod_ref, ow_ref, wo_ref, o_ref):
    kr = r_ref.shape[1]
    an = _rms(a_ref[...].astype(F32), ow_ref[...]).astype(BF16)
    mixed = jnp.dot(r_ref[...], wo_ref[:kr, :], preferred_element_type=F32)
    mixed = mixed + jnp.dot(an, wo_ref[kr:, :], preferred_element_type=F32)
    o_ref[...] = x_ref[...] + mod_ref[2:3, :] * mixed


def _oproj(ret_out, attn, x2d, mod, ow, w_o, seq):
    t, d = x2d.shape
    tm = min(512, seq)
    per_b = seq // tm
    kr = ret_out.shape[1]
    ka = attn.shape[1]
    return pl.pallas_call(
        _oproj_kernel,
        grid=(t // tm,),
        in_specs=[pl.BlockSpec((tm, kr), lambda i: (i, 0)),
                  pl.BlockSpec((tm, ka), lambda i: (i, 0)),
                  pl.BlockSpec((tm, d), lambda i: (i, 0)),
                  pl.BlockSpec((None, N_MOD, d), lambda i: (i // per_b, 0, 0)),
                  pl.BlockSpec((1, ka), lambda i: (0, 0)),
                  pl.BlockSpec((kr + ka, d), lambda i: (0, 0))],
        out_specs=pl.BlockSpec((tm, d), lambda i: (i, 0)),
        out_shape=jax.ShapeDtypeStruct((t, d), F32),
        compiler_params=_cparams(("parallel",), 48),
        name="oproj",
    )(ret_out, attn, x2d, mod, ow, w_o)


def _ffn_kernel(x_ref, mod_ref, n2_ref, wg_ref, wu_ref, wd_ref, fw_ref, o_ref, h_ref, acc_ref):
    f = pl.program_id(1)

    @pl.when(f == 0)
    def _():
        h = _rms(x_ref[...], n2_ref[...]) * (1.0 + mod_ref[4:5, :]) + mod_ref[3:4, :]
        h_ref[...] = h.astype(BF16)
        acc_ref[...] = jnp.zeros_like(acc_ref)

    h = h_ref[...]
    g = jnp.dot(h, wg_ref[...], preferred_element_type=F32)
    u = jnp.dot(h, wu_ref[...], preferred_element_type=F32)
    a = (_silu(g) * u).astype(BF16)
    acc_ref[...] += jnp.dot(a, wd_ref[...], preferred_element_type=F32)

    @pl.when(f == pl.num_programs(1) - 1)
    def _():
        x2 = x_ref[...] + mod_ref[5:6, :] * acc_ref[...]
        o_ref[...] = _rms(x2, fw_ref[...])


def _ffn(x2d, mod, n2w, wg, wu, wd, fw, seq):
    t, d = x2d.shape
    dff = wg.shape[1]
    tm = min(512, seq)
    tf = 512
    per_b = seq // tm
    row = pl.BlockSpec((1, d), lambda i, f: (0, 0))
    return pl.pallas_call(
        _ffn_kernel,
        grid=(t // tm, dff // tf),
        in_specs=[pl.BlockSpec((tm, d), lambda i, f: (i, 0)),
                  pl.BlockSpec((None, N_MOD, d), lambda i, f: (i // per_b, 0, 0)),
                  row,
                  pl.BlockSpec((d, tf), lambda i, f: (0, f)),
                  pl.BlockSpec((d, tf), lambda i, f: (0, f)),
                  pl.BlockSpec((tf, d), lambda i, f: (f, 0)),
                  row],
        out_specs=pl.BlockSpec((tm, d), lambda i, f: (i, 0)),
        out_shape=jax.ShapeDtypeStruct((t, d), F32),
        scratch_shapes=[pltpu.VMEM((tm, d), BF16), pltpu.VMEM((tm, d), F32)],
        compiler_params=_cparams(("parallel", "arbitrary"), 52),
        name="ffn",
    )(x2d, mod, n2w, wg, wu, wd, fw)


def _rope_freq_row():
    fr = ROPE_BASE ** (-jnp.arange(0, RET_HEAD_DIM, 2, dtype=F32) / RET_HEAD_DIM)
    fm = ROPE_BASE ** (-jnp.arange(0, MLA_ROPE_DIM, 2, dtype=F32) / MLA_ROPE_DIM)
    pad = jnp.zeros((LANES - fr.shape[0] - fm.shape[0],), F32)
    return jnp.concatenate([fr, fm, pad]).reshape(1, LANES)


def kernel(x, c, positions, ada_w, ada_b, norm1_w, w_in, ret_decay, ret_gn_w, ret_gn_b, mla_q_norm_w, w_uq,
           mla_kv_norm_w, w_ukv, mla_out_w, w_o, norm2_w, w_gate, w_up, w_down, final_norm_w):
    bsz, seq, d = x.shape
    t = bsz * seq
    assert ada_w.shape[0] == 1, "the final norm is fused into the (single) layer's FFN kernel"
    ret_w = RET_HEADS * RET_HEAD_DIM
    n_main = 4 * ret_w + MLA_Q_RANK + MLA_KV_RANK

    tabs = _rope_tables(positions.reshape(t, 1), _rope_freq_row())
    xc = x.reshape(t, d)
    mod = _adaln(c, ada_w[0], ada_b[0]).reshape(bsz, N_MOD, d)

    w_main = w_in[0][:, :n_main].astype(BF16)
    w_rope = jnp.pad(w_in[0][:, n_main:], ((0, 0), (0, LANES - MLA_ROPE_DIM))).astype(BF16)
    proj, k_rope = _inproj(xc, mod, norm1_w[0].reshape(1, d), w_main, w_rope, tabs,
                           mla_q_norm_w[0].reshape(1, -1), mla_kv_norm_w[0].reshape(1, -1), seq)

    wq = w_uq[0].reshape(MLA_Q_RANK, MLA_HEADS, MLA_QK_DIM)
    wq = jnp.pad(wq, ((0, 0), (0, 0), (0, MLA_QK_PAD - MLA_QK_DIM)))
    w_uq_p = wq.reshape(MLA_Q_RANK, MLA_HEADS * MLA_QK_PAD).astype(BF16)
    wkv = w_ukv[0].reshape(MLA_KV_RANK, MLA_HEADS, MLA_NOPE_DIM + MLA_V_DIM)
    w_ukv_p = jnp.concatenate(
        [wkv[:, :, :MLA_NOPE_DIM].reshape(MLA_KV_RANK, -1), wkv[:, :, MLA_NOPE_DIM:].reshape(MLA_KV_RANK, -1)],
        axis=1).astype(BF16)
    q_m, kv_m = _mla_up(proj, w_uq_p, w_ukv_p, tabs[2:], seq)
    attn = _attention(q_m, kv_m, k_rope, bsz, seq)

    log_gamma = -jnp.exp(ret_decay[0].astype(F32))
    ret_out = _retention(log_gamma, proj, ret_gn_w[0].reshape(1, -1), ret_gn_b[0].reshape(1, -1), bsz, seq)

    x1 = _oproj(ret_out, attn, xc, mod, mla_out_w[0].reshape(1, -1), w_o[0].astype(BF16), seq)
    out = _ffn(x1, mod, norm2_w[0].reshape(1, d), w_gate[0].astype(BF16), w_up[0].astype(BF16),
               w_down[0].astype(BF16), final_norm_w.reshape(1, d), seq)
    return out.reshape(bsz, seq, d)
```

```python
import functools

import jax
import jax.numpy as jnp
from jax import lax
from jax.experimental import pallas as pl
from jax.experimental.pallas import tpu as pltpu

F32 = jnp.float32
BF16 = jnp.bfloat16

EPS = 1e-6
ROPE_BASE = 10000.0
N_MOD = 6
RET_HEADS = 8
RET_HEAD_DIM = 128
RET_CHUNK = 128
RET_UNROLL = 8
MLA_HEADS = 8
MLA_NOPE_DIM = 128
MLA_ROPE_DIM = 64
MLA_V_DIM = 128
MLA_QK_DIM = MLA_NOPE_DIM + MLA_ROPE_DIM
MLA_Q_RANK = 512
MLA_KV_RANK = 512
LANES = 128
MLA_QK_PAD = 2 * LANES
MIB = 1024 * 1024
LOG2E = 1.4426950408889634

_NT = (((1,), (1,)), ((), ()))


def _cparams(sem, vmem_mib):
    return pltpu.CompilerParams(dimension_semantics=sem, vmem_limit_bytes=int(vmem_mib * MIB))


def _rms(x, w):
    return x * lax.rsqrt(jnp.mean(x * x, axis=-1, keepdims=True) + EPS) * w


def _silu(x):
    return x * jax.nn.sigmoid(x)


ROW_CHUNK = 16


def _for_row_chunks(nrows, fn):
    def body(r, carry):
        fn(pl.ds(pl.multiple_of(r * ROW_CHUNK, ROW_CHUNK), ROW_CHUNK))
        return carry

    lax.fori_loop(0, nrows // ROW_CHUNK, body, 0, unroll=8)


def _adaln_kernel(c_ref, w_ref, b_ref, o_ref):
    ca = _silu(c_ref[...]).astype(BF16)
    o_ref[...] = jnp.dot(ca, w_ref[...].astype(BF16), preferred_element_type=F32) + b_ref[...]


def _adaln(c, w, b):
    bsz, d = c.shape
    n = w.shape[1]
    tn = 1024
    return pl.pallas_call(
        _adaln_kernel,
        grid=(n // tn,),
        in_specs=[pl.BlockSpec((bsz, d), lambda j: (0, 0)),
                  pl.BlockSpec((d, tn), lambda j: (0, j)),
                  pl.BlockSpec((1, tn), lambda j: (0, j))],
        out_specs=pl.BlockSpec((bsz, tn), lambda j: (0, j)),
        out_shape=jax.ShapeDtypeStruct((bsz, n), F32),
        compiler_params=_cparams(("parallel",), 40),
        name="adaln",
    )(c, w, b.reshape(1, n))


def _rope_table_kernel(pos_ref, frq_ref, cr_ref, sr_ref, cm_ref, sa_ref, sb_ref):
    ang = pos_ref[...].astype(F32) * frq_ref[...]
    cs = jnp.cos(ang)
    sn = jnp.sin(ang)
    lane = lax.broadcasted_iota(jnp.int32, ang.shape, 1)
    cs64 = pltpu.roll(cs, 64, 1)
    sn64 = pltpu.roll(sn, 64, 1)
    cs96 = pltpu.roll(cs, 96, 1)
    sn96 = pltpu.roll(sn, 96, 1)
    cr_ref[...] = jnp.where(lane < 64, cs, cs64)
    sr_ref[...] = jnp.where(lane < 64, -sn, sn64)
    cm_ref[...] = jnp.where(lane < 32, cs64, cs96)
    sa_ref[...] = jnp.where(lane < 32, -sn64, 0.0)
    sb_ref[...] = jnp.where((lane >= 32) & (lane < 64), sn96, 0.0)


def _rope_tables(pos_col, frq):
    t = pos_col.shape[0]
    tr = min(1024, t)
    spec = pl.BlockSpec((tr, LANES), lambda i: (i, 0))
    return pl.pallas_call(
        _rope_table_kernel,
        grid=(t // tr,),
        in_specs=[pl.BlockSpec((tr, 1), lambda i: (i, 0)),
                  pl.BlockSpec((1, LANES), lambda i: (0, 0))],
        out_specs=[spec] * 5,
        out_shape=[jax.ShapeDtypeStruct((t, LANES), F32)] * 5,
        compiler_params=_cparams(("parallel",), 32),
        name="rope_tables",
    )(pos_col, frq)


def _rope128(t, c, s):
    return t * c + pltpu.roll(t, 64, 1) * s


def _rope64(t, cm, sa, sb):
    return t * cm + pltpu.roll(t, 96, 1) * sa + pltpu.roll(t, 32, 1) * sb


def _inproj_kernel(x_ref, mod_ref, n1_ref, w_ref, wr_ref, cr_ref, sr_ref, cm_ref, sa_ref, sb_ref,
                   qnw_ref, kvnw_ref, proj_ref, kr_ref, h_ref, *, k_scale):
    j = pl.program_id(1)
    nheads = proj_ref.shape[1] // LANES

    @pl.when(j == 0)
    def _():
        def prologue(rows):
            h = _rms(x_ref[rows, :], n1_ref[...]) * (1.0 + mod_ref[1:2, :]) + mod_ref[0:1, :]
            h_ref[rows, :] = h.astype(BF16)

        _for_row_chunks(x_ref.shape[0], prologue)

    def tile():
        return jnp.dot(h_ref[...], w_ref[...], preferred_element_type=F32)

    def rope_store(scale):
        acc = tile()
        c = cr_ref[...]
        s = sr_ref[...]
        for hh in range(nheads):
            sl = slice(hh * LANES, (hh + 1) * LANES)
            o = _rope128(acc[:, sl], c, s)
            if scale is not None:
                o = o * scale
            proj_ref[:, sl] = o.astype(BF16)

    @pl.when(j == 0)
    def _():
        rope_store(None)

    @pl.when(j == 1)
    def _():
        rope_store(k_scale)

    @pl.when((j == 2) | (j == 3))
    def _():
        proj_ref[...] = tile().astype(BF16)

    @pl.when(j == 4)
    def _():
        acc = tile()
        r = MLA_Q_RANK
        proj_ref[:, :r] = _rms(acc[:, :r], qnw_ref[...]).astype(BF16)
        proj_ref[:, r:] = _rms(acc[:, r:], kvnw_ref[...]).astype(BF16)
        t = jnp.dot(h_ref[...], wr_ref[...], preferred_element_type=F32)
        kr_ref[...] = _rope64(t, cm_ref[...], sa_ref[...], sb_ref[...]).astype(BF16)


def _inproj(x2d, mod, n1w, w_all, n, w_rope, tabs, qnw, kvnw, seq):
    t, d = x2d.shape
    tn = 1024
    assert n == 5 * tn and MLA_Q_RANK + MLA_KV_RANK == tn and w_all.shape[1] >= n
    tm = min(1024, seq)
    per_b = seq // tm
    tab_spec = pl.BlockSpec((tm, LANES), lambda i, j: (i, 0))
    row = lambda width: pl.BlockSpec((1, width), lambda i, j: (0, 0))
    return pl.pallas_call(
        functools.partial(_inproj_kernel, k_scale=RET_HEAD_DIM ** -0.5),
        grid=(t // tm, n // tn),
        in_specs=[pl.BlockSpec((tm, d), lambda i, j: (i, 0)),
                  pl.BlockSpec((None, N_MOD, d), lambda i, j: (i // per_b, 0, 0)),
                  row(d),
                  pl.BlockSpec((d, tn), lambda i, j: (0, j)),
                  pl.BlockSpec((d, LANES), lambda i, j: (0, 0)),
                  tab_spec, tab_spec, tab_spec, tab_spec, tab_spec,
                  row(MLA_Q_RANK), row(MLA_KV_RANK)],
        out_specs=[pl.BlockSpec((tm, tn), lambda i, j: (i, j)),
                   pl.BlockSpec((tm, LANES), lambda i, j: (i, 0))],
        out_shape=[jax.ShapeDtypeStruct((t, n), BF16),
                   jax.ShapeDtypeStruct((t, LANES), BF16)],
        scratch_shapes=[pltpu.VMEM((tm, d), BF16)],
        compiler_params=_cparams(("parallel", "arbitrary"), 56),
        name="inproj",
    )(x2d, mod, n1w, w_all, w_rope, *tabs, qnw, kvnw)


def _mla_up_kernel(cq_ref, ckv_ref, wq_ref, wkv_ref, cm_ref, sa_ref, sb_ref, q_ref, kv_ref, *, q_scale):
    q = jnp.dot(cq_ref[...], wq_ref[...], preferred_element_type=F32)
    cm = cm_ref[...]
    sa = sa_ref[...]
    sb = sb_ref[...]
    for hh in range(MLA_HEADS):
        base = hh * MLA_QK_PAD
        q_ref[:, base:base + LANES] = (q[:, base:base + LANES] * q_scale).astype(BF16)
        t = q[:, base + LANES:base + 2 * LANES]
        q_ref[:, base + LANES:base + 2 * LANES] = (_rope64(t, cm, sa, sb) * q_scale).astype(BF16)
    kv_ref[...] = jnp.dot(ckv_ref[...], wkv_ref[...], preferred_element_type=F32).astype(BF16)


def _mla_up(proj, w_uq_p, w_ukv_p, tabs_m, seq):
    t = proj.shape[0]
    tm = min(1024, seq)
    r = MLA_Q_RANK
    cq_blk = (4 * RET_HEADS * RET_HEAD_DIM) // r
    nq = w_uq_p.shape[1]
    nkv = w_ukv_p.shape[1]
    tab_spec = pl.BlockSpec((tm, LANES), lambda i: (i, 0))
    return pl.pallas_call(
        functools.partial(_mla_up_kernel, q_scale=MLA_QK_DIM ** -0.5 * LOG2E),
        grid=(t // tm,),
        in_specs=[pl.BlockSpec((tm, r), lambda i: (i, cq_blk)),
                  pl.BlockSpec((tm, r), lambda i: (i, cq_blk + 1)),
                  pl.BlockSpec((r, nq), lambda i: (0, 0)),
                  pl.BlockSpec((r, nkv), lambda i: (0, 0)),
                  tab_spec, tab_spec, tab_spec],
        out_specs=[pl.BlockSpec((tm, nq), lambda i: (i, 0)),
                   pl.BlockSpec((tm, nkv), lambda i: (i, 0))],
        out_shape=[jax.ShapeDtypeStruct((t, nq), BF16),
                   jax.ShapeDtypeStruct((t, nkv), BF16)],
        compiler_params=_cparams(("parallel",), 48),
        name="mla_up",
    )(proj, proj, w_uq_p, w_ukv_p, *tabs_m)


def _attn_kernel(q_ref, kn_ref, kr_ref, v_ref, o_ref, kcat_ref, vt_ref, s0_ref, s1_ref, m0_ref, m1_ref, *, tq, tk):
    seq = q_ref.shape[0]
    nq = seq // tq
    nk = seq // tk
    kcat_ref[:, :LANES] = kn_ref[...]
    kcat_ref[:, LANES:] = kr_ref[...]
    vt_ref[...] = v_ref[...].astype(F32).T.astype(BF16)
    s_refs = (s0_ref, s1_ref)
    m_refs = (m0_ref, m1_ref)

    def scores_chunk(q, slot, c, m):
        ks = slice(c * tk, (c + 1) * tk)
        s = lax.dot_general(kcat_ref[ks, :], q, _NT, preferred_element_type=F32)
        mc = jnp.max(s, axis=0, keepdims=True)
        s_refs[slot][ks, :] = s
        return mc if m is None else jnp.maximum(m, mc)

    def finish_chunk(slot, c, m, l, acc):
        ks = slice(c * tk, (c + 1) * tk)
        p = jnp.exp2(s_refs[slot][ks, :] - m)
        lc = jnp.sum(p, axis=0, keepdims=True)
        ac = jnp.dot(vt_ref[:, ks], p.astype(BF16), preferred_element_type=F32)
        return (lc if l is None else l + lc), (ac if acc is None else acc + ac)

    def step(i, slot, do_scores, do_finish):
        if do_scores:
            qn = q_ref[pl.ds(pl.multiple_of((i + 1) * tq, tq), tq), :]
        if do_finish:
            mcur = m_refs[slot][...]
        m = l = acc = None
        for c in range(nk):
            if do_scores:
                m = scores_chunk(qn, 1 - slot, c, m)
            if do_finish:
                l, acc = finish_chunk(slot, c, mcur, l, acc)
        if do_scores:
            m_refs[1 - slot][...] = m
        if do_finish:
            rows = pl.ds(pl.multiple_of(i * tq, tq), tq)
            o_ref[rows, :] = (acc / l).T.astype(o_ref.dtype)

    step(-1, 1, True, False)

    def pair(j, carry):
        step(2 * j, 0, True, True)
        step(2 * j + 1, 1, True, True)
        return carry

    lax.fori_loop(0, nq // 2 - 1, pair, 0)
    step(nq - 2, 0, True, True)
    step(nq - 1, 1, False, True)


def _attention(q, kv, kr, bsz, seq):
    t = q.shape[0]
    tq = min(512, seq // 2)
    tk = min(512, seq)
    assert seq % (2 * tq) == 0 and seq % tk == 0
    kv_blk = lambda off: pl.BlockSpec((seq, LANES), lambda b, h: (b, off + h))
    return pl.pallas_call(
        functools.partial(_attn_kernel, tq=tq, tk=tk),
        grid=(bsz, MLA_HEADS),
        in_specs=[pl.BlockSpec((seq, MLA_QK_PAD), lambda b, h: (b, h)),
                  kv_blk(0),
                  pl.BlockSpec((seq, LANES), lambda b, h: (b, 0)),
                  kv_blk(MLA_HEADS)],
        out_specs=pl.BlockSpec((seq, MLA_V_DIM), lambda b, h: (b, h)),
        out_shape=jax.ShapeDtypeStruct((t, MLA_HEADS * MLA_V_DIM), BF16),
        scratch_shapes=[pltpu.VMEM((seq, MLA_QK_PAD), BF16),
                        pltpu.VMEM((MLA_V_DIM, seq), BF16),
                        pltpu.VMEM((seq, tq), F32), pltpu.VMEM((seq, tq), F32),
                        pltpu.VMEM((1, tq), F32), pltpu.VMEM((1, tq), F32)],
        compiler_params=_cparams(("parallel", "parallel"), 56),
        name="mla_attn",
    )(q, kv, kr, kv)


def _ret_kernel(lg_ref, q_ref, k_ref, v_ref, g_ref, gw_ref, gb_ref, o_ref, p_ref, kv_ref, st_ref):
    h = pl.program_id(1)
    lgf = lg_ref[0, h]
    lgb = lg_ref[1, h]
    c = RET_CHUNK
    d = q_ref.shape[1]
    nc = q_ref.shape[0] // c
    ii = lax.broadcasted_iota(jnp.int32, (c, c), 0).astype(F32)
    jj = lax.broadcasted_iota(jnp.int32, (c, c), 1).astype(F32)
    diff = ii - jj
    dmat = jnp.where(diff >= 0, jnp.exp(jnp.maximum(diff, 0.0) * lgf), jnp.exp(jnp.maximum(-diff, 0.0) * lgb))
    qpos = lax.broadcasted_iota(jnp.int32, (c, d), 0).astype(F32)
    kpos = lax.broadcasted_iota(jnp.int32, (d, c), 1).astype(F32)
    xi_f = jnp.exp((qpos + 1.0) * lgf)
    xi_b = jnp.exp((c - qpos) * lgb)
    zeta_f = jnp.exp((c - 1.0 - kpos) * lgf)
    zeta_b = jnp.exp(kpos * lgb)
    dec_f = jnp.exp(jnp.full((d, d), c, F32) * lgf)
    dec_b = jnp.exp(jnp.full((d, d), c, F32) * lgb)

    def chunk_rows(n):
        return pl.ds(pl.multiple_of(n * c, c), c)

    def chunk_sums(n, carry):
        rows = chunk_rows(n)
        kt = k_ref[rows, :].astype(F32).T
        lhs = jnp.concatenate([(kt * zeta_f).astype(BF16), (kt * zeta_b).astype(BF16)], axis=0)
        kv_ref[n] = jnp.dot(lhs, v_ref[rows, :], preferred_element_type=F32)
        s = jnp.dot(q_ref[rows, :], kt.astype(BF16), preferred_element_type=F32) * dmat
        p_ref[rows, :] = s.astype(BF16)
        return carry

    lax.fori_loop(0, nc, chunk_sums, 0, unroll=RET_UNROLL)

    def scan(n, carry):
        sf, sb = carry
        nb = nc - 1 - n
        st_ref[n, :d, :] = sf.astype(BF16)
        st_ref[nb, d:, :] = sb.astype(BF16)
        return sf * dec_f + kv_ref[n, :d, :], sb * dec_b + kv_ref[nb, d:, :]

    zero = jnp.zeros((d, d), F32)
    lax.fori_loop(0, nc, scan, (zero, zero), unroll=RET_UNROLL)

    def outputs(n, carry):
        rows = chunk_rows(n)
        qf = q_ref[rows, :].astype(F32)
        qx = jnp.concatenate([(qf * xi_f).astype(BF16), (qf * xi_b).astype(BF16)], axis=1)
        y = jnp.dot(p_ref[rows, :], v_ref[rows, :], preferred_element_type=F32)
        y = y + jnp.dot(qx, st_ref[n], preferred_element_type=F32)
        mu = jnp.mean(y, axis=-1, keepdims=True)
        yc = y - mu
        var = jnp.mean(yc * yc, axis=-1, keepdims=True)
        yn = yc * lax.rsqrt(var + EPS) * gw_ref[...] + gb_ref[...]
        o_ref[rows, :] = (_silu(g_ref[rows, :].astype(F32)) * yn).astype(o_ref.dtype)
        return carry

    lax.fori_loop(0, nc, outputs, 0, unroll=RET_UNROLL)


def _retention(log_gamma, proj, gn_w, gn_b, bsz, seq):
    t = proj.shape[0]
    hd = RET_HEAD_DIM
    nh = RET_HEADS
    blk = lambda off: pl.BlockSpec((seq, hd), lambda b, h: (b, off + h))
    row = pl.BlockSpec((1, hd), lambda b, h: (0, h))
    return pl.pallas_call(
        _ret_kernel,
        grid=(bsz, nh),
        in_specs=[pl.BlockSpec(memory_space=pltpu.SMEM),
                  blk(0), blk(nh), blk(2 * nh), blk(3 * nh), row, row],
        out_specs=pl.BlockSpec((seq, hd), lambda b, h: (b, h)),
        out_shape=jax.ShapeDtypeStruct((t, nh * hd), BF16),
        scratch_shapes=[pltpu.VMEM((seq, RET_CHUNK), BF16),
                        pltpu.VMEM((seq // RET_CHUNK, 2 * hd, hd), F32),
                        pltpu.VMEM((seq // RET_CHUNK, 2 * hd, hd), BF16)],
        compiler_params=_cparams(("parallel", "parallel"), 32),
        name="retention",
    )(log_gamma, proj, proj, proj, proj, gn_w, gn_b)


def _oproj_kernel(r_ref, a_ref, x_ref, mod_ref, ow_ref, wo_ref, o_ref):
    kr = r_ref.shape[1]
    an = _rms(a_ref[...].astype(F32), ow_ref[...]).astype(BF16)
    mixed = jnp.dot(r_ref[...], wo_ref[:kr, :], preferred_element_type=F32)
    mixed = mixed + jnp.dot(an, wo_ref[kr:, :], preferred_element_type=F32)
    o_ref[...] = x_ref[...] + mod_ref[2:3, :] * mixed


def _oproj(ret_out, attn, x2d, mod, ow, w_o, seq):
    t, d = x2d.shape
    tm = min(512, seq)
    per_b = seq // tm
    kr = ret_out.shape[1]
    ka = attn.shape[1]
    return pl.pallas_call(
        _oproj_kernel,
        grid=(t // tm,),
        in_specs=[pl.BlockSpec((tm, kr), lambda i: (i, 0)),
                  pl.BlockSpec((tm, ka), lambda i: (i, 0)),
                  pl.BlockSpec((tm, d), lambda i: (i, 0)),
                  pl.BlockSpec((None, N_MOD, d), lambda i: (i // per_b, 0, 0)),
                  pl.BlockSpec((1, ka), lambda i: (0, 0)),
                  pl.BlockSpec((kr + ka, d), lambda i: (0, 0))],
        out_specs=pl.BlockSpec((tm, d), lambda i: (i, 0)),
        out_shape=jax.ShapeDtypeStruct((t, d), F32),
        compiler_params=_cparams(("parallel",), 48),
        name="oproj",
    )(ret_out, attn, x2d, mod, ow, w_o)


def _ffn_kernel(x_ref, mod_ref, n2_ref, wg_ref, wu_ref, wd_ref, fw_ref, o_ref, h_ref, acc_ref):
    f = pl.program_id(1)

    @pl.when(f == 0)
    def _():
        def prologue(rows):
            h = _rms(x_ref[rows, :], n2_ref[...]) * (1.0 + mod_ref[4:5, :]) + mod_ref[3:4, :]
            h_ref[rows, :] = h.astype(BF16)
            acc_ref[rows, :] = jnp.zeros((ROW_CHUNK, acc_ref.shape[1]), F32)

        _for_row_chunks(x_ref.shape[0], prologue)

    h = h_ref[...]
    g = jnp.dot(h, wg_ref[...], preferred_element_type=F32)
    u = jnp.dot(h, wu_ref[...], preferred_element_type=F32)
    a = (_silu(g) * u).astype(BF16)
    acc_ref[...] += jnp.dot(a, wd_ref[...], preferred_element_type=F32)

    @pl.when(f == pl.num_programs(1) - 1)
    def _():
        def epilogue(rows):
            x2 = x_ref[rows, :] + mod_ref[5:6, :] * acc_ref[rows, :]
            o_ref[rows, :] = _rms(x2, fw_ref[...])

        _for_row_chunks(x_ref.shape[0], epilogue)


def _ffn(x2d, mod, n2w, wg, wu, wd, fw, seq):
    t, d = x2d.shape
    dff = wg.shape[1]
    tm = min(512, seq)
    tf = 512
    per_b = seq // tm
    row = pl.BlockSpec((1, d), lambda i, f: (0, 0))
    return pl.pallas_call(
        _ffn_kernel,
        grid=(t // tm, dff // tf),
        in_specs=[pl.BlockSpec((tm, d), lambda i, f: (i, 0)),
                  pl.BlockSpec((None, N_MOD, d), lambda i, f: (i // per_b, 0, 0)),
                  row,
                  pl.BlockSpec((d, tf), lambda i, f: (0, f)),
                  pl.BlockSpec((d, tf), lambda i, f: (0, f)),
                  pl.BlockSpec((tf, d), lambda i, f: (f, 0)),
                  row],
        out_specs=pl.BlockSpec((tm, d), lambda i, f: (i, 0)),
        out_shape=jax.ShapeDtypeStruct((t, d), F32),
        scratch_shapes=[pltpu.VMEM((tm, d), BF16), pltpu.VMEM((tm, d), F32)],
        compiler_params=_cparams(("parallel", "arbitrary"), 52),
        name="ffn",
    )(x2d, mod, n2w, wg, wu, wd, fw)


def _rope_freq_row():
    fr = ROPE_BASE ** (-jnp.arange(0, RET_HEAD_DIM, 2, dtype=F32) / RET_HEAD_DIM)
    fm = ROPE_BASE ** (-jnp.arange(0, MLA_ROPE_DIM, 2, dtype=F32) / MLA_ROPE_DIM)
    pad = jnp.zeros((LANES - fr.shape[0] - fm.shape[0],), F32)
    return jnp.concatenate([fr, fm, pad]).reshape(1, LANES)


def kernel(x, c, positions, ada_w, ada_b, norm1_w, w_in, ret_decay, ret_gn_w, ret_gn_b, mla_q_norm_w, w_uq,
           mla_kv_norm_w, w_ukv, mla_out_w, w_o, norm2_w, w_gate, w_up, w_down, final_norm_w):
    bsz, seq, d = x.shape
    t = bsz * seq
    assert ada_w.shape[0] == 1, "the final norm is fused into the (single) layer's FFN kernel"
    ret_w = RET_HEADS * RET_HEAD_DIM
    n_main = 4 * ret_w + MLA_Q_RANK + MLA_KV_RANK

    tabs = _rope_tables(positions.reshape(t, 1), _rope_freq_row())
    xc = x.reshape(t, d)
    drop_depth = lambda w: w.reshape(w.shape[1:])
    mod = _adaln(c, drop_depth(ada_w), ada_b[0]).reshape(bsz, N_MOD, d)

    w_all = drop_depth(w_in).astype(BF16)
    w_rope = jnp.pad(w_all[:, n_main:], ((0, 0), (0, LANES - MLA_ROPE_DIM)))
    proj, k_rope = _inproj(xc, mod, norm1_w[0].reshape(1, d), w_all, n_main, w_rope, tabs,
                           mla_q_norm_w[0].reshape(1, -1), mla_kv_norm_w[0].reshape(1, -1), seq)

    wq = w_uq[0].reshape(MLA_Q_RANK, MLA_HEADS, MLA_QK_DIM)
    wq = jnp.pad(wq, ((0, 0), (0, 0), (0, MLA_QK_PAD - MLA_QK_DIM)))
    w_uq_p = wq.reshape(MLA_Q_RANK, MLA_HEADS * MLA_QK_PAD).astype(BF16)
    wkv = w_ukv[0].reshape(MLA_KV_RANK, MLA_HEADS, MLA_NOPE_DIM + MLA_V_DIM)
    w_ukv_p = jnp.concatenate(
        [wkv[:, :, :MLA_NOPE_DIM].reshape(MLA_KV_RANK, -1), wkv[:, :, MLA_NOPE_DIM:].reshape(MLA_KV_RANK, -1)],
        axis=1).astype(BF16)
    q_m, kv_m = _mla_up(proj, w_uq_p, w_ukv_p, tabs[2:], seq)
    attn = _attention(q_m, kv_m, k_rope, bsz, seq)

    log_gamma = -jnp.exp(ret_decay[0].astype(F32))
    ret_out = _retention(log_gamma, proj, ret_gn_w[0].reshape(1, -1), ret_gn_b[0].reshape(1, -1), bsz, seq)

    x1 = _oproj(ret_out, attn, xc, mod, mla_out_w[0].reshape(1, -1), drop_depth(w_o).astype(BF16), seq)
    out = _ffn(x1, mod, norm2_w[0].reshape(1, d), drop_depth(w_gate).astype(BF16), drop_depth(w_up).astype(BF16),
               drop_depth(w_down).astype(BF16), final_norm_w.reshape(1, d), seq)
    return out.reshape(bsz, seq, d)
```

```python
import functools

import jax
import jax.numpy as jnp
from jax import lax
from jax.experimental import pallas as pl
from jax.experimental.pallas import tpu as pltpu

F32 = jnp.float32
BF16 = jnp.bfloat16

EPS = 1e-6
ROPE_BASE = 10000.0
N_MOD = 6
RET_HEADS = 8
RET_HEAD_DIM = 128
RET_CHUNK = 128
RET_UNROLL = 8
MLA_HEADS = 8
MLA_NOPE_DIM = 128
MLA_ROPE_DIM = 64
MLA_V_DIM = 128
MLA_QK_DIM = MLA_NOPE_DIM + MLA_ROPE_DIM
MLA_Q_RANK = 512
MLA_KV_RANK = 512
LANES = 128
MLA_QK_PAD = 2 * LANES
MIB = 1024 * 1024
LOG2E = 1.4426950408889634

_NT = (((1,), (1,)), ((), ()))


def _cparams(sem, vmem_mib):
    return pltpu.CompilerParams(dimension_semantics=sem, vmem_limit_bytes=int(vmem_mib * MIB))


def _rms(x, w):
    return x * lax.rsqrt(jnp.mean(x * x, axis=-1, keepdims=True) + EPS) * w


def _silu(x):
    return x * jax.nn.sigmoid(x)


def _adaln_kernel(c_ref, w_ref, b_ref, o_ref):
    ca = _silu(c_ref[...]).astype(BF16)
    o_ref[...] = jnp.dot(ca, w_ref[...].astype(BF16), preferred_element_type=F32) + b_ref[...]


def _adaln(c, w, b):
    bsz, d = c.shape
    n = w.shape[1]
    tn = 1024
    return pl.pallas_call(
        _adaln_kernel,
        grid=(n // tn,),
        in_specs=[pl.BlockSpec((bsz, d), lambda j: (0, 0)),
                  pl.BlockSpec((d, tn), lambda j: (0, j)),
                  pl.BlockSpec((1, tn), lambda j: (0, j))],
        out_specs=pl.BlockSpec((bsz, tn), lambda j: (0, j)),
        out_shape=jax.ShapeDtypeStruct((bsz, n), F32),
        compiler_params=_cparams(("parallel",), 40),
        name="adaln",
    )(c, w, b.reshape(1, n))


def _rope_table_kernel(pos_ref, frq_ref, cr_ref, sr_ref, cm_ref, sa_ref, sb_ref):
    ang = pos_ref[...].astype(F32) * frq_ref[...]
    cs = jnp.cos(ang)
    sn = jnp.sin(ang)
    lane = lax.broadcasted_iota(jnp.int32, ang.shape, 1)
    cs64 = pltpu.roll(cs, 64, 1)
    sn64 = pltpu.roll(sn, 64, 1)
    cs96 = pltpu.roll(cs, 96, 1)
    sn96 = pltpu.roll(sn, 96, 1)
    cr_ref[...] = jnp.where(lane < 64, cs, cs64)
    sr_ref[...] = jnp.where(lane < 64, -sn, sn64)
    cm_ref[...] = jnp.where(lane < 32, cs64, cs96)
    sa_ref[...] = jnp.where(lane < 32, -sn64, 0.0)
    sb_ref[...] = jnp.where((lane >= 32) & (lane < 64), sn96, 0.0)


def _rope_tables(pos_col, frq):
    t = pos_col.shape[0]
    tr = min(1024, t)
    spec = pl.BlockSpec((tr, LANES), lambda i: (i, 0))
    return pl.pallas_call(
        _rope_table_kernel,
        grid=(t // tr,),
        in_specs=[pl.BlockSpec((tr, 1), lambda i: (i, 0)),
                  pl.BlockSpec((1, LANES), lambda i: (0, 0))],
        out_specs=[spec] * 5,
        out_shape=[jax.ShapeDtypeStruct((t, LANES), F32)] * 5,
        compiler_params=_cparams(("parallel",), 32),
        name="rope_tables",
    )(pos_col, frq)


def _rope128(t, c, s):
    return t * c + pltpu.roll(t, 64, 1) * s


def _rope64(t, cm, sa, sb):
    return t * cm + pltpu.roll(t, 96, 1) * sa + pltpu.roll(t, 32, 1) * sb


def _inproj_kernel(x_ref, mod_ref, n1_ref, w_ref, wr_ref, cr_ref, sr_ref, cm_ref, sa_ref, sb_ref,
                   proj_ref, kr_ref, h_ref, *, k_scale):
    j = pl.program_id(1)
    nheads = proj_ref.shape[1] // LANES

    @pl.when(j == 0)
    def _():
        h = _rms(x_ref[...], n1_ref[...]) * (1.0 + mod_ref[1:2, :]) + mod_ref[0:1, :]
        h_ref[...] = h.astype(BF16)

    def tile():
        return jnp.dot(h_ref[...], w_ref[...], preferred_element_type=F32)

    def rope_store(scale):
        acc = tile()
        c = cr_ref[...]
        s = sr_ref[...]
        for hh in range(nheads):
            sl = slice(hh * LANES, (hh + 1) * LANES)
            o = _rope128(acc[:, sl], c, s)
            if scale is not None:
                o = o * scale
            proj_ref[:, sl] = o.astype(BF16)

    @pl.when(j == 0)
    def _():
        rope_store(None)

    @pl.when(j == 1)
    def _():
        rope_store(k_scale)

    @pl.when((j == 2) | (j == 3))
    def _():
        proj_ref[...] = tile().astype(BF16)

    @pl.when(j == 4)
    def _():
        proj_ref[...] = tile().astype(BF16)
        t = jnp.dot(h_ref[...], wr_ref[...], preferred_element_type=F32)
        kr_ref[...] = _rope64(t, cm_ref[...], sa_ref[...], sb_ref[...]).astype(BF16)


def _inproj(x2d, mod, n1w, w_all, n, w_rope, tabs, seq):
    t, d = x2d.shape
    tn = 1024
    assert n == 5 * tn and MLA_Q_RANK + MLA_KV_RANK == tn and w_all.shape[1] >= n
    tm = min(1024, seq)
    per_b = seq // tm
    tab_spec = pl.BlockSpec((tm, LANES), lambda i, j: (i, 0))
    row = lambda width: pl.BlockSpec((1, width), lambda i, j: (0, 0))
    return pl.pallas_call(
        functools.partial(_inproj_kernel, k_scale=RET_HEAD_DIM ** -0.5),
        grid=(t // tm, n // tn),
        in_specs=[pl.BlockSpec((tm, d), lambda i, j: (i, 0)),
                  pl.BlockSpec((None, N_MOD, d), lambda i, j: (i // per_b, 0, 0)),
                  row(d),
                  pl.BlockSpec((d, tn), lambda i, j: (0, j)),
                  pl.BlockSpec((d, LANES), lambda i, j: (0, 0)),
                  tab_spec, tab_spec, tab_spec, tab_spec, tab_spec],
        out_specs=[pl.BlockSpec((tm, tn), lambda i, j: (i, j)),
                   pl.BlockSpec((tm, LANES), lambda i, j: (i, 0))],
        out_shape=[jax.ShapeDtypeStruct((t, n), BF16),
                   jax.ShapeDtypeStruct((t, LANES), BF16)],
        scratch_shapes=[pltpu.VMEM((tm, d), BF16)],
        compiler_params=_cparams(("parallel", "arbitrary"), 56),
        name="inproj",
    )(x2d, mod, n1w, w_all, w_rope, *tabs)


def _mla_up_kernel(cq_ref, ckv_ref, qnw_ref, kvnw_ref, wq_ref, wkv_ref, cm_ref, sa_ref, sb_ref, q_ref, kv_ref, *,
                   q_scale):
    cq = _rms(cq_ref[...].astype(F32), qnw_ref[...]).astype(BF16)
    ckv = _rms(ckv_ref[...].astype(F32), kvnw_ref[...]).astype(BF16)
    q = jnp.dot(cq, wq_ref[...], preferred_element_type=F32)
    cm = cm_ref[...]
    sa = sa_ref[...]
    sb = sb_ref[...]
    for hh in range(MLA_HEADS):
        base = hh * MLA_QK_PAD
        q_ref[:, base:base + LANES] = (q[:, base:base + LANES] * q_scale).astype(BF16)
        t = q[:, base + LANES:base + 2 * LANES]
        q_ref[:, base + LANES:base + 2 * LANES] = (_rope64(t, cm, sa, sb) * q_scale).astype(BF16)
    kv_ref[...] = jnp.dot(ckv, wkv_ref[...], preferred_element_type=F32).astype(BF16)


def _mla_up(proj, qnw, kvnw, w_uq_p, w_ukv_p, tabs_m, seq):
    t = proj.shape[0]
    tm = min(1024, seq)
    r = MLA_Q_RANK
    cq_blk = (4 * RET_HEADS * RET_HEAD_DIM) // r
    nq = w_uq_p.shape[1]
    nkv = w_ukv_p.shape[1]
    tab_spec = pl.BlockSpec((tm, LANES), lambda i: (i, 0))
    return pl.pallas_call(
        functools.partial(_mla_up_kernel, q_scale=MLA_QK_DIM ** -0.5 * LOG2E),
        grid=(t // tm,),
        in_specs=[pl.BlockSpec((tm, r), lambda i: (i, cq_blk)),
                  pl.BlockSpec((tm, r), lambda i: (i, cq_blk + 1)),
                  pl.BlockSpec((1, r), lambda i: (0, 0)),
                  pl.BlockSpec((1, r), lambda i: (0, 0)),
                  pl.BlockSpec((r, nq), lambda i: (0, 0)),
                  pl.BlockSpec((r, nkv), lambda i: (0, 0)),
                  tab_spec, tab_spec, tab_spec],
        out_specs=[pl.BlockSpec((tm, nq), lambda i: (i, 0)),
                   pl.BlockSpec((tm, nkv), lambda i: (i, 0))],
        out_shape=[jax.ShapeDtypeStruct((t, nq), BF16),
                   jax.ShapeDtypeStruct((t, nkv), BF16)],
        compiler_params=_cparams(("parallel",), 48),
        name="mla_up",
    )(proj, proj, qnw, kvnw, w_uq_p, w_ukv_p, *tabs_m)


def _attn_kernel(q_ref, kn_ref, kr_ref, v_ref, o_ref, kcat_ref, vt_ref, s0_ref, s1_ref, m0_ref, m1_ref, *, tq, tk):
    seq = q_ref.shape[0]
    nq = seq // tq
    nk = seq // tk
    kcat_ref[:, :LANES] = kn_ref[...]
    kcat_ref[:, LANES:] = kr_ref[...]
    vt_ref[...] = v_ref[...].astype(F32).T.astype(BF16)
    s_refs = (s0_ref, s1_ref)
    m_refs = (m0_ref, m1_ref)

    def scores_chunk(q, slot, c, m):
        ks = slice(c * tk, (c + 1) * tk)
        s = lax.dot_general(kcat_ref[ks, :], q, _NT, preferred_element_type=F32)
        mc = jnp.max(s, axis=0, keepdims=True)
        s_refs[slot][ks, :] = s
        return mc if m is None else jnp.maximum(m, mc)

    def finish_chunk(slot, c, m, l, acc):
        ks = slice(c * tk, (c + 1) * tk)
        p = jnp.exp2(s_refs[slot][ks, :] - m)
        lc = jnp.sum(p, axis=0, keepdims=True)
        ac = jnp.dot(vt_ref[:, ks], p.astype(BF16), preferred_element_type=F32)
        return (lc if l is None else l + lc), (ac if acc is None else acc + ac)

    def step(i, slot, do_scores, do_finish):
        if do_scores:
            qn = q_ref[pl.ds(pl.multiple_of((i + 1) * tq, tq), tq), :]
        if do_finish:
            mcur = m_refs[slot][...]
        m = l = acc = None
        for c in range(nk):
            if do_scores:
                m = scores_chunk(qn, 1 - slot, c, m)
            if do_finish:
                l, acc = finish_chunk(slot, c, mcur, l, acc)
        if do_scores:
            m_refs[1 - slot][...] = m
        if do_finish:
            rows = pl.ds(pl.multiple_of(i * tq, tq), tq)
            o_ref[rows, :] = (acc / l).T.astype(o_ref.dtype)

    step(-1, 1, True, False)

    def pair(j, carry):
        step(2 * j, 0, True, True)
        step(2 * j + 1, 1, True, True)
        return carry

    lax.fori_loop(0, nq // 2 - 1, pair, 0)
    step(nq - 2, 0, True, True)
    step(nq - 1, 1, False, True)


def _attention(q, kv, kr, bsz, seq):
    t = q.shape[0]
    tq = min(512, seq // 2)
    tk = min(1024, seq)
    assert seq % (2 * tq) == 0 and seq % tk == 0
    kv_blk = lambda off: pl.BlockSpec((seq, LANES), lambda b, h: (b, off + h))
    return pl.pallas_call(
        functools.partial(_attn_kernel, tq=tq, tk=tk),
        grid=(bsz, MLA_HEADS),
        in_specs=[pl.BlockSpec((seq, MLA_QK_PAD), lambda b, h: (b, h)),
                  kv_blk(0),
                  pl.BlockSpec((seq, LANES), lambda b, h: (b, 0)),
                  kv_blk(MLA_HEADS)],
        out_specs=pl.BlockSpec((seq, MLA_V_DIM), lambda b, h: (b, h)),
        out_shape=jax.ShapeDtypeStruct((t, MLA_HEADS * MLA_V_DIM), BF16),
        scratch_shapes=[pltpu.VMEM((seq, MLA_QK_PAD), BF16),
                        pltpu.VMEM((MLA_V_DIM, seq), BF16),
                        pltpu.VMEM((seq, tq), F32), pltpu.VMEM((seq, tq), F32),
                        pltpu.VMEM((1, tq), F32), pltpu.VMEM((1, tq), F32)],
        compiler_params=_cparams(("parallel", "parallel"), 56),
        name="mla_attn",
    )(q, kv, kr, kv)


def _ret_kernel(lg_ref, q_ref, k_ref, v_ref, g_ref, gw_ref, gb_ref, o_ref, p_ref, kv_ref, st_ref):
    h = pl.program_id(1)
    lgf = lg_ref[0, h]
    lgb = lg_ref[1, h]
    c = RET_CHUNK
    d = q_ref.shape[1]
    nc = q_ref.shape[0] // c
    ii = lax.broadcasted_iota(jnp.int32, (c, c), 0).astype(F32)
    jj = lax.broadcasted_iota(jnp.int32, (c, c), 1).astype(F32)
    diff = ii - jj
    dmat = jnp.where(diff >= 0, jnp.exp(jnp.maximum(diff, 0.0) * lgf), jnp.exp(jnp.maximum(-diff, 0.0) * lgb))
    qpos = lax.broadcasted_iota(jnp.int32, (c, d), 0).astype(F32)
    kpos = lax.broadcasted_iota(jnp.int32, (d, c), 1).astype(F32)
    xi_f = jnp.exp((qpos + 1.0) * lgf)
    xi_b = jnp.exp((c - qpos) * lgb)
    zeta_f = jnp.exp((c - 1.0 - kpos) * lgf)
    zeta_b = jnp.exp(kpos * lgb)
    dec_f = jnp.exp(jnp.full((d, d), c, F32) * lgf)
    dec_b = jnp.exp(jnp.full((d, d), c, F32) * lgb)

    def chunk_rows(n):
        return pl.ds(pl.multiple_of(n * c, c), c)

    def chunk_sums(n, carry):
        rows = chunk_rows(n)
        kt = k_ref[rows, :].astype(F32).T
        lhs = jnp.concatenate([(kt * zeta_f).astype(BF16), (kt * zeta_b).astype(BF16)], axis=0)
        kv_ref[n] = jnp.dot(lhs, v_ref[rows, :], preferred_element_type=F32)
        s = jnp.dot(q_ref[rows, :], kt.astype(BF16), preferred_element_type=F32) * dmat
        p_ref[rows, :] = s.astype(BF16)
        return carry

    lax.fori_loop(0, nc, chunk_sums, 0, unroll=RET_UNROLL)

    def scan(n, carry):
        sf, sb = carry
        nb = nc - 1 - n
        st_ref[n, :d, :] = sf.astype(BF16)
        st_ref[nb, d:, :] = sb.astype(BF16)
        return sf * dec_f + kv_ref[n, :d, :], sb * dec_b + kv_ref[nb, d:, :]

    zero = jnp.zeros((d, d), F32)
    lax.fori_loop(0, nc, scan, (zero, zero), unroll=RET_UNROLL)

    def outputs(n, carry):
        rows = chunk_rows(n)
        qf = q_ref[rows, :].astype(F32)
        qx = jnp.concatenate([(qf * xi_f).astype(BF16), (qf * xi_b).astype(BF16)], axis=1)
        y = jnp.dot(p_ref[rows, :], v_ref[rows, :], preferred_element_type=F32)
        y = y + jnp.dot(qx, st_ref[n], preferred_element_type=F32)
        mu = jnp.mean(y, axis=-1, keepdims=True)
        yc = y - mu
        var = jnp.mean(yc * yc, axis=-1, keepdims=True)
        yn = yc * lax.rsqrt(var + EPS) * gw_ref[...] + gb_ref[...]
        o_ref[rows, :] = (_silu(g_ref[rows, :].astype(F32)) * yn).astype(o_ref.dtype)
        return carry

    lax.fori_loop(0, nc, outputs, 0, unroll=RET_UNROLL)


def _retention(log_gamma, proj, gn_w, gn_b, bsz, seq):
    t = proj.shape[0]
    hd = RET_HEAD_DIM
    nh = RET_HEADS
    blk = lambda off: pl.BlockSpec((seq, hd), lambda b, h: (b, off + h))
    row = pl.BlockSpec((1, hd), lambda b, h: (0, h))
    return pl.pallas_call(
        _ret_kernel,
        grid=(bsz, nh),
        in_specs=[pl.BlockSpec(memory_space=pltpu.SMEM),
                  blk(0), blk(nh), blk(2 * nh), blk(3 * nh), row, row],
        out_specs=pl.BlockSpec((seq, hd), lambda b, h: (b, h)),
        out_shape=jax.ShapeDtypeStruct((t, nh * hd), BF16),
        scratch_shapes=[pltpu.VMEM((seq, RET_CHUNK), BF16),
                        pltpu.VMEM((seq // RET_CHUNK, 2 * hd, hd), F32),
                        pltpu.VMEM((seq // RET_CHUNK, 2 * hd, hd), BF16)],
        compiler_params=_cparams(("parallel", "parallel"), 32),
        name="retention",
    )(log_gamma, proj, proj, proj, proj, gn_w, gn_b)


def _oproj_kernel(r_ref, a_ref, x_ref, mod_ref, ow_ref, wo_ref, o_ref):
    kr = r_ref.shape[1]
    an = _rms(a_ref[...].astype(F32), ow_ref[...]).astype(BF16)
    mixed = jnp.dot(r_ref[...], wo_ref[:kr, :], preferred_element_type=F32)
    mixed = mixed + jnp.dot(an, wo_ref[kr:, :], preferred_element_type=F32)
    o_ref[...] = x_ref[...] + mod_ref[2:3, :] * mixed


def _oproj(ret_out, attn, x2d, mod, ow, w_o, seq):
    t, d = x2d.shape
    tm = min(512, seq)
    per_b = seq // tm
    kr = ret_out.shape[1]
    ka = attn.shape[1]
    return pl.pallas_call(
        _oproj_kernel,
        grid=(t // tm,),
        in_specs=[pl.BlockSpec((tm, kr), lambda i: (i, 0)),
                  pl.BlockSpec((tm, ka), lambda i: (i, 0)),
                  pl.BlockSpec((tm, d), lambda i: (i, 0)),
                  pl.BlockSpec((None, N_MOD, d), lambda i: (i // per_b, 0, 0)),
                  pl.BlockSpec((1, ka), lambda i: (0, 0)),
                  pl.BlockSpec((kr + ka, d), lambda i: (0, 0))],
        out_specs=pl.BlockSpec((tm, d), lambda i: (i, 0)),
        out_shape=jax.ShapeDtypeStruct((t, d), F32),
        compiler_params=_cparams(("parallel",), 48),
        name="oproj",
    )(ret_out, attn, x2d, mod, ow, w_o)


def _ffn_kernel(x_ref, mod_ref, n2_ref, wg_ref, wu_ref, wd_ref, fw_ref, o_ref, h_ref, acc_ref):
    f = pl.program_id(1)
    last = pl.num_programs(1) - 1

    def swiglu_down(h):
        g = jnp.dot(h, wg_ref[...], preferred_element_type=F32)
        u = jnp.dot(h, wu_ref[...], preferred_element_type=F32)
        a = (_silu(g) * u).astype(BF16)
        return jnp.dot(a, wd_ref[...], preferred_element_type=F32)

    @pl.when(f == 0)
    def _():
        h = (_rms(x_ref[...], n2_ref[...]) * (1.0 + mod_ref[4:5, :]) + mod_ref[3:4, :]).astype(BF16)
        h_ref[...] = h
        acc_ref[...] = swiglu_down(h)

    @pl.when((f > 0) & (f < last))
    def _():
        acc_ref[...] += swiglu_down(h_ref[...])

    @pl.when(f == last)
    def _():
        x2 = x_ref[...] + mod_ref[5:6, :] * (acc_ref[...] + swiglu_down(h_ref[...]))
        o_ref[...] = _rms(x2, fw_ref[...])


def _ffn(x2d, mod, n2w, wg, wu, wd, fw, seq):
    t, d = x2d.shape
    dff = wg.shape[1]
    tm = min(512, seq)
    tf = 512
    per_b = seq // tm
    row = pl.BlockSpec((1, d), lambda i, f: (0, 0))
    return pl.pallas_call(
        _ffn_kernel,
        grid=(t // tm, dff // tf),
        in_specs=[pl.BlockSpec((tm, d), lambda i, f: (i, 0)),
                  pl.BlockSpec((None, N_MOD, d), lambda i, f: (i // per_b, 0, 0)),
                  row,
                  pl.BlockSpec((d, tf), lambda i, f: (0, f)),
                  pl.BlockSpec((d, tf), lambda i, f: (0, f)),
                  pl.BlockSpec((tf, d), lambda i, f: (f, 0)),
                  row],
        out_specs=pl.BlockSpec((tm, d), lambda i, f: (i, 0)),
        out_shape=jax.ShapeDtypeStruct((t, d), F32),
        scratch_shapes=[pltpu.VMEM((tm, d), BF16), pltpu.VMEM((tm, d), F32)],
        compiler_params=_cparams(("parallel", "arbitrary"), 52),
        name="ffn",
    )(x2d, mod, n2w, wg, wu, wd, fw)


def _rope_freq_row():
    fr = ROPE_BASE ** (-jnp.arange(0, RET_HEAD_DIM, 2, dtype=F32) / RET_HEAD_DIM)
    fm = ROPE_BASE ** (-jnp.arange(0, MLA_ROPE_DIM, 2, dtype=F32) / MLA_ROPE_DIM)
    pad = jnp.zeros((LANES - fr.shape[0] - fm.shape[0],), F32)
    return jnp.concatenate([fr, fm, pad]).reshape(1, LANES)


def kernel(x, c, positions, ada_w, ada_b, norm1_w, w_in, ret_decay, ret_gn_w, ret_gn_b, mla_q_norm_w, w_uq,
           mla_kv_norm_w, w_ukv, mla_out_w, w_o, norm2_w, w_gate, w_up, w_down, final_norm_w):
    bsz, seq, d = x.shape
    t = bsz * seq
    assert ada_w.shape[0] == 1, "the final norm is fused into the (single) layer's FFN kernel"
    ret_w = RET_HEADS * RET_HEAD_DIM
    n_main = 4 * ret_w + MLA_Q_RANK + MLA_KV_RANK

    tabs = _rope_tables(positions.reshape(t, 1), _rope_freq_row())
    xc = x.reshape(t, d)
    drop_depth = lambda w: w.reshape(w.shape[1:])
    mod = _adaln(c, drop_depth(ada_w), ada_b[0]).reshape(bsz, N_MOD, d)

    w_all = drop_depth(w_in).astype(BF16)
    w_rope = jnp.pad(w_all[:, n_main:], ((0, 0), (0, LANES - MLA_ROPE_DIM)))
    proj, k_rope = _inproj(xc, mod, norm1_w[0].reshape(1, d), w_all, n_main, w_rope, tabs, seq)

    wq = w_uq[0].reshape(MLA_Q_RANK, MLA_HEADS, MLA_QK_DIM)
    wq = jnp.pad(wq, ((0, 0), (0, 0), (0, MLA_QK_PAD - MLA_QK_DIM)))
    w_uq_p = wq.reshape(MLA_Q_RANK, MLA_HEADS * MLA_QK_PAD).astype(BF16)
    wkv = w_ukv[0].reshape(MLA_KV_RANK, MLA_HEADS, MLA_NOPE_DIM + MLA_V_DIM)
    w_ukv_p = jnp.concatenate(
        [wkv[:, :, :MLA_NOPE_DIM].reshape(MLA_KV_RANK, -1), wkv[:, :, MLA_NOPE_DIM:].reshape(MLA_KV_RANK, -1)],
        axis=1).astype(BF16)
    q_m, kv_m = _mla_up(proj, mla_q_norm_w[0].reshape(1, -1), mla_kv_norm_w[0].reshape(1, -1), w_uq_p, w_ukv_p,
                        tabs[2:], seq)
    attn = _attention(q_m, kv_m, k_rope, bsz, seq)

    log_gamma = -jnp.exp(ret_decay[0].astype(F32))
    ret_out = _retention(log_gamma, proj, ret_gn_w[0].reshape(1, -1), ret_gn_b[0].reshape(1, -1), bsz, seq)

    x1 = _oproj(ret_out, attn, xc, mod, mla_out_w[0].reshape(1, -1), drop_depth(w_o).astype(BF16), seq)
    out = _ffn(x1, mod, norm2_w[0].reshape(1, d), drop_depth(w_gate).astype(BF16), drop_depth(w_up).astype(BF16),
               drop_depth(w_down).astype(BF16), final_norm_w.reshape(1, d), seq)
    return out.reshape(bsz, seq, d)
```

```python
import functools

import jax
import jax.numpy as jnp
from jax import lax
from jax.experimental import pallas as pl
from jax.experimental.pallas import tpu as pltpu

F32 = jnp.float32
BF16 = jnp.bfloat16

EPS = 1e-6
ROPE_BASE = 10000.0
N_MOD = 6
RET_HEADS = 8
RET_HEAD_DIM = 128
RET_CHUNK = 128
RET_UNROLL = 8
MLA_HEADS = 8
MLA_NOPE_DIM = 128
MLA_ROPE_DIM = 64
MLA_V_DIM = 128
MLA_QK_DIM = MLA_NOPE_DIM + MLA_ROPE_DIM
MLA_Q_RANK = 512
MLA_KV_RANK = 512
LANES = 128
MLA_QK_PAD = 2 * LANES
MIB = 1024 * 1024
LOG2E = 1.4426950408889634

_NT = (((1,), (1,)), ((), ()))


def _cparams(sem, vmem_mib):
    return pltpu.CompilerParams(dimension_semantics=sem, vmem_limit_bytes=int(vmem_mib * MIB))


def _rms(x, w):
    return x * lax.rsqrt(jnp.mean(x * x, axis=-1, keepdims=True) + EPS) * w


def _silu(x):
    return x * jax.nn.sigmoid(x)


def _adaln_kernel(c_ref, w_ref, b_ref, o_ref):
    ca = _silu(c_ref[...]).astype(BF16)
    o_ref[...] = jnp.dot(ca, w_ref[...].astype(BF16), preferred_element_type=F32) + b_ref[...]


def _adaln(c, w, b):
    bsz, d = c.shape
    n = w.shape[1]
    tn = 1024
    return pl.pallas_call(
        _adaln_kernel,
        grid=(n // tn,),
        in_specs=[pl.BlockSpec((bsz, d), lambda j: (0, 0)),
                  pl.BlockSpec((d, tn), lambda j: (0, j)),
                  pl.BlockSpec((1, tn), lambda j: (0, j))],
        out_specs=pl.BlockSpec((bsz, tn), lambda j: (0, j)),
        out_shape=jax.ShapeDtypeStruct((bsz, n), F32),
        compiler_params=_cparams(("parallel",), 40),
        name="adaln",
    )(c, w, b.reshape(1, n))


def _rope_table_kernel(pos_ref, frq_ref, cr_ref, sr_ref, cm_ref, sa_ref, sb_ref):
    ang = pos_ref[...].astype(F32) * frq_ref[...]
    cs = jnp.cos(ang)
    sn = jnp.sin(ang)
    lane = lax.broadcasted_iota(jnp.int32, ang.shape, 1)
    cs64 = pltpu.roll(cs, 64, 1)
    sn64 = pltpu.roll(sn, 64, 1)
    cs96 = pltpu.roll(cs, 96, 1)
    sn96 = pltpu.roll(sn, 96, 1)
    cr_ref[...] = jnp.where(lane < 64, cs, cs64)
    sr_ref[...] = jnp.where(lane < 64, -sn, sn64)
    cm_ref[...] = jnp.where(lane < 32, cs64, cs96)
    sa_ref[...] = jnp.where(lane < 32, -sn64, 0.0)
    sb_ref[...] = jnp.where((lane >= 32) & (lane < 64), sn96, 0.0)


def _rope_tables(pos_col, frq):
    t = pos_col.shape[0]
    tr = min(1024, t)
    spec = pl.BlockSpec((tr, LANES), lambda i: (i, 0))
    return pl.pallas_call(
        _rope_table_kernel,
        grid=(t // tr,),
        in_specs=[pl.BlockSpec((tr, 1), lambda i: (i, 0)),
                  pl.BlockSpec((1, LANES), lambda i: (0, 0))],
        out_specs=[spec] * 5,
        out_shape=[jax.ShapeDtypeStruct((t, LANES), F32)] * 5,
        compiler_params=_cparams(("parallel",), 32),
        name="rope_tables",
    )(pos_col, frq)


def _rope128(t, c, s):
    return t * c + pltpu.roll(t, 64, 1) * s


def _rope64(t, cm, sa, sb):
    return t * cm + pltpu.roll(t, 96, 1) * sa + pltpu.roll(t, 32, 1) * sb


def _inproj_kernel(x_ref, mod_ref, n1_ref, w_ref, wr_ref, cr_ref, sr_ref, cm_ref, sa_ref, sb_ref,
                   proj_ref, kr_ref, h_ref, *, k_scale):
    j = pl.program_id(1)
    nheads = proj_ref.shape[1] // LANES

    @pl.when(j == 0)
    def _():
        h = _rms(x_ref[...], n1_ref[...]) * (1.0 + mod_ref[1:2, :]) + mod_ref[0:1, :]
        h_ref[...] = h.astype(BF16)

    def tile():
        return jnp.dot(h_ref[...], w_ref[...], preferred_element_type=F32)

    def rope_store(scale):
        acc = tile()
        c = cr_ref[...]
        s = sr_ref[...]
        for hh in range(nheads):
            sl = slice(hh * LANES, (hh + 1) * LANES)
            o = _rope128(acc[:, sl], c, s)
            if scale is not None:
                o = o * scale
            proj_ref[:, sl] = o.astype(BF16)

    @pl.when(j == 0)
    def _():
        rope_store(None)

    @pl.when(j == 1)
    def _():
        rope_store(k_scale)

    @pl.when((j == 2) | (j == 3))
    def _():
        proj_ref[...] = tile().astype(BF16)

    @pl.when(j == 4)
    def _():
        proj_ref[...] = tile().astype(BF16)
        t = jnp.dot(h_ref[...], wr_ref[...], preferred_element_type=F32)
        kr_ref[...] = _rope64(t, cm_ref[...], sa_ref[...], sb_ref[...]).astype(BF16)


def _inproj(x2d, mod, n1w, w_all, n, w_rope, tabs, seq):
    t, d = x2d.shape
    tn = 1024
    assert n == 5 * tn and MLA_Q_RANK + MLA_KV_RANK == tn and w_all.shape[1] >= n
    tm = min(1024, seq)
    per_b = seq // tm
    tab_spec = pl.BlockSpec((tm, LANES), lambda i, j: (i, 0))
    row = lambda width: pl.BlockSpec((1, width), lambda i, j: (0, 0))
    return pl.pallas_call(
        functools.partial(_inproj_kernel, k_scale=RET_HEAD_DIM ** -0.5),
        grid=(t // tm, n // tn),
        in_specs=[pl.BlockSpec((tm, d), lambda i, j: (i, 0)),
                  pl.BlockSpec((None, N_MOD, d), lambda i, j: (i // per_b, 0, 0)),
                  row(d),
                  pl.BlockSpec((d, tn), lambda i, j: (0, j)),
                  pl.BlockSpec((d, LANES), lambda i, j: (0, 0)),
                  tab_spec, tab_spec, tab_spec, tab_spec, tab_spec],
        out_specs=[pl.BlockSpec((tm, tn), lambda i, j: (i, j)),
                   pl.BlockSpec((tm, LANES), lambda i, j: (i, 0))],
        out_shape=[jax.ShapeDtypeStruct((t, n), BF16),
                   jax.ShapeDtypeStruct((t, LANES), BF16)],
        scratch_shapes=[pltpu.VMEM((tm, d), BF16)],
        compiler_params=_cparams(("parallel", "arbitrary"), 56),
        name="inproj",
    )(x2d, mod, n1w, w_all, w_rope, *tabs)


def _mla_up_kernel(cq_ref, ckv_ref, qnw_ref, kvnw_ref, wq_ref, wkv_ref, cm_ref, sa_ref, sb_ref, q_ref, kv_ref, *,
                   q_scale):
    cq = _rms(cq_ref[...].astype(F32), qnw_ref[...]).astype(BF16)
    ckv = _rms(ckv_ref[...].astype(F32), kvnw_ref[...]).astype(BF16)
    q = jnp.dot(cq, wq_ref[...], preferred_element_type=F32)
    cm = cm_ref[...]
    sa = sa_ref[...]
    sb = sb_ref[...]
    for hh in range(MLA_HEADS):
        base = hh * MLA_QK_PAD
        q_ref[:, base:base + LANES] = (q[:, base:base + LANES] * q_scale).astype(BF16)
        t = q[:, base + LANES:base + 2 * LANES]
        q_ref[:, base + LANES:base + 2 * LANES] = (_rope64(t, cm, sa, sb) * q_scale).astype(BF16)
    kv_ref[...] = jnp.dot(ckv, wkv_ref[...], preferred_element_type=F32).astype(BF16)


def _mla_up(proj, qnw, kvnw, w_uq_p, w_ukv_p, tabs_m, seq):
    t = proj.shape[0]
    tm = min(1024, seq)
    r = MLA_Q_RANK
    cq_blk = (4 * RET_HEADS * RET_HEAD_DIM) // r
    nq = w_uq_p.shape[1]
    nkv = w_ukv_p.shape[1]
    tab_spec = pl.BlockSpec((tm, LANES), lambda i: (i, 0))
    return pl.pallas_call(
        functools.partial(_mla_up_kernel, q_scale=MLA_QK_DIM ** -0.5 * LOG2E),
        grid=(t // tm,),
        in_specs=[pl.BlockSpec((tm, r), lambda i: (i, cq_blk)),
                  pl.BlockSpec((tm, r), lambda i: (i, cq_blk + 1)),
                  pl.BlockSpec((1, r), lambda i: (0, 0)),
                  pl.BlockSpec((1, r), lambda i: (0, 0)),
                  pl.BlockSpec((r, nq), lambda i: (0, 0)),
                  pl.BlockSpec((r, nkv), lambda i: (0, 0)),
                  tab_spec, tab_spec, tab_spec],
        out_specs=[pl.BlockSpec((tm, nq), lambda i: (i, 0)),
                   pl.BlockSpec((tm, nkv), lambda i: (i, 0))],
        out_shape=[jax.ShapeDtypeStruct((t, nq), BF16),
                   jax.ShapeDtypeStruct((t, nkv), BF16)],
        compiler_params=_cparams(("parallel",), 48),
        name="mla_up",
    )(proj, proj, qnw, kvnw, w_uq_p, w_ukv_p, *tabs_m)


def _attn_kernel(q_ref, kn_ref, kr_ref, v_ref, o_ref, kcat_ref, vt_ref, s0_ref, s1_ref, m0_ref, m1_ref, *, tq, tk):
    seq = q_ref.shape[0]
    nq = seq // tq
    nk = seq // tk
    kcat_ref[:, :LANES] = kn_ref[...]
    kcat_ref[:, LANES:] = kr_ref[...]
    vt_ref[...] = v_ref[...].astype(F32).T.astype(BF16)
    s_refs = (s0_ref, s1_ref)
    m_refs = (m0_ref, m1_ref)

    def scores_chunk(q, slot, c, m):
        ks = slice(c * tk, (c + 1) * tk)
        s = lax.dot_general(kcat_ref[ks, :], q, _NT, preferred_element_type=F32)
        mc = jnp.max(s, axis=0, keepdims=True)
        s_refs[slot][ks, :] = s
        return mc if m is None else jnp.maximum(m, mc)

    def finish_chunk(slot, c, m, l, acc):
        ks = slice(c * tk, (c + 1) * tk)
        p = jnp.exp2(s_refs[slot][ks, :] - m)
        lc = jnp.sum(p, axis=0, keepdims=True)
        ac = jnp.dot(vt_ref[:, ks], p.astype(BF16), preferred_element_type=F32)
        return (lc if l is None else l + lc), (ac if acc is None else acc + ac)

    def step(i, slot, do_scores, do_finish):
        if do_scores:
            qn = q_ref[pl.ds(pl.multiple_of((i + 1) * tq, tq), tq), :]
        if do_finish:
            mcur = m_refs[slot][...]
        m = l = acc = None
        for c in range(nk):
            if do_scores:
                m = scores_chunk(qn, 1 - slot, c, m)
            if do_finish:
                l, acc = finish_chunk(slot, c, mcur, l, acc)
        if do_scores:
            m_refs[1 - slot][...] = m
        if do_finish:
            rows = pl.ds(pl.multiple_of(i * tq, tq), tq)
            o_ref[rows, :] = (acc / l).T.astype(o_ref.dtype)

    step(-1, 1, True, False)

    def pair(j, carry):
        step(2 * j, 0, True, True)
        step(2 * j + 1, 1, True, True)
        return carry

    lax.fori_loop(0, nq // 2 - 1, pair, 0)
    step(nq - 2, 0, True, True)
    step(nq - 1, 1, False, True)


def _attention(q, kv, kr, bsz, seq):
    t = q.shape[0]
    tq = min(512, seq // 2)
    tk = min(2048, seq)
    assert seq % (2 * tq) == 0 and seq % tk == 0
    kv_blk = lambda off: pl.BlockSpec((seq, LANES), lambda b, h: (b, off + h))
    return pl.pallas_call(
        functools.partial(_attn_kernel, tq=tq, tk=tk),
        grid=(bsz, MLA_HEADS),
        in_specs=[pl.BlockSpec((seq, MLA_QK_PAD), lambda b, h: (b, h)),
                  kv_blk(0),
                  pl.BlockSpec((seq, LANES), lambda b, h: (b, 0)),
                  kv_blk(MLA_HEADS)],
        out_specs=pl.BlockSpec((seq, MLA_V_DIM), lambda b, h: (b, h)),
        out_shape=jax.ShapeDtypeStruct((t, MLA_HEADS * MLA_V_DIM), BF16),
        scratch_shapes=[pltpu.VMEM((seq, MLA_QK_PAD), BF16),
                        pltpu.VMEM((MLA_V_DIM, seq), BF16),
                        pltpu.VMEM((seq, tq), F32), pltpu.VMEM((seq, tq), F32),
                        pltpu.VMEM((1, tq), F32), pltpu.VMEM((1, tq), F32)],
        compiler_params=_cparams(("parallel", "parallel"), 56),
        name="mla_attn",
    )(q, kv, kr, kv)


def _ret_kernel(lg_ref, q_ref, k_ref, v_ref, g_ref, gw_ref, gb_ref, o_ref, p_ref, kv_ref, st_ref):
    h = pl.program_id(1)
    lgf = lg_ref[0, h]
    lgb = lg_ref[1, h]
    c = RET_CHUNK
    d = q_ref.shape[1]
    nc = q_ref.shape[0] // c
    ii = lax.broadcasted_iota(jnp.int32, (c, c), 0).astype(F32)
    jj = lax.broadcasted_iota(jnp.int32, (c, c), 1).astype(F32)
    diff = ii - jj
    dmat = jnp.where(diff >= 0, jnp.exp(jnp.maximum(diff, 0.0) * lgf), jnp.exp(jnp.maximum(-diff, 0.0) * lgb))
    qpos = lax.broadcasted_iota(jnp.int32, (c, d), 0).astype(F32)
    kpos = lax.broadcasted_iota(jnp.int32, (d, c), 1).astype(F32)
    xi_f = jnp.exp((qpos + 1.0) * lgf)
    xi_b = jnp.exp((c - qpos) * lgb)
    zeta_f = jnp.exp((c - 1.0 - kpos) * lgf)
    zeta_b = jnp.exp(kpos * lgb)
    dec_f = jnp.exp(jnp.full((d, d), c, F32) * lgf)
    dec_b = jnp.exp(jnp.full((d, d), c, F32) * lgb)

    def chunk_rows(n):
        return pl.ds(pl.multiple_of(n * c, c), c)

    def chunk_sums(n, carry):
        rows = chunk_rows(n)
        kt = k_ref[rows, :].astype(F32).T
        lhs = jnp.concatenate([(kt * zeta_f).astype(BF16), (kt * zeta_b).astype(BF16)], axis=0)
        kv_ref[n] = jnp.dot(lhs, v_ref[rows, :], preferred_element_type=F32)
        s = jnp.dot(q_ref[rows, :], kt.astype(BF16), preferred_element_type=F32) * dmat
        p_ref[rows, :] = s.astype(BF16)
        return carry

    lax.fori_loop(0, nc, chunk_sums, 0, unroll=RET_UNROLL)

    def scan(n, carry):
        sf, sb = carry
        nb = nc - 1 - n
        st_ref[n, :d, :] = sf.astype(BF16)
        st_ref[nb, d:, :] = sb.astype(BF16)
        return sf * dec_f + kv_ref[n, :d, :], sb * dec_b + kv_ref[nb, d:, :]

    zero = jnp.zeros((d, d), F32)
    lax.fori_loop(0, nc, scan, (zero, zero), unroll=RET_UNROLL)

    def outputs(n, carry):
        rows = chunk_rows(n)
        qf = q_ref[rows, :].astype(F32)
        qx = jnp.concatenate([(qf * xi_f).astype(BF16), (qf * xi_b).astype(BF16)], axis=1)
        y = jnp.dot(p_ref[rows, :], v_ref[rows, :], preferred_element_type=F32)
        y = y + jnp.dot(qx, st_ref[n], preferred_element_type=F32)
        mu = jnp.mean(y, axis=-1, keepdims=True)
        yc = y - mu
        var = jnp.mean(yc * yc, axis=-1, keepdims=True)
        yn = yc * lax.rsqrt(var + EPS) * gw_ref[...] + gb_ref[...]
        o_ref[rows, :] = (_silu(g_ref[rows, :].astype(F32)) * yn).astype(o_ref.dtype)
        return carry

    lax.fori_loop(0, nc, outputs, 0, unroll=RET_UNROLL)


def _retention(log_gamma, proj, gn_w, gn_b, bsz, seq):
    t = proj.shape[0]
    hd = RET_HEAD_DIM
    nh = RET_HEADS
    blk = lambda off: pl.BlockSpec((seq, hd), lambda b, h: (b, off + h))
    row = pl.BlockSpec((1, hd), lambda b, h: (0, h))
    return pl.pallas_call(
        _ret_kernel,
        grid=(bsz, nh),
        in_specs=[pl.BlockSpec(memory_space=pltpu.SMEM),
                  blk(0), blk(nh), blk(2 * nh), blk(3 * nh), row, row],
        out_specs=pl.BlockSpec((seq, hd), lambda b, h: (b, h)),
        out_shape=jax.ShapeDtypeStruct((t, nh * hd), BF16),
        scratch_shapes=[pltpu.VMEM((seq, RET_CHUNK), BF16),
                        pltpu.VMEM((seq // RET_CHUNK, 2 * hd, hd), F32),
                        pltpu.VMEM((seq // RET_CHUNK, 2 * hd, hd), BF16)],
        compiler_params=_cparams(("parallel", "parallel"), 32),
        name="retention",
    )(log_gamma, proj, proj, proj, proj, gn_w, gn_b)


def _oproj_kernel(r_ref, a_ref, x_ref, mod_ref, ow_ref, wo_ref, o_ref):
    kr = r_ref.shape[1]
    an = _rms(a_ref[...].astype(F32), ow_ref[...]).astype(BF16)
    mixed = jnp.dot(r_ref[...], wo_ref[:kr, :], preferred_element_type=F32)
    mixed = mixed + jnp.dot(an, wo_ref[kr:, :], preferred_element_type=F32)
    o_ref[...] = x_ref[...] + mod_ref[2:3, :] * mixed


def _oproj(ret_out, attn, x2d, mod, ow, w_o, seq):
    t, d = x2d.shape
    tm = min(512, seq)
    per_b = seq // tm
    kr = ret_out.shape[1]
    ka = attn.shape[1]
    return pl.pallas_call(
        _oproj_kernel,
        grid=(t // tm,),
        in_specs=[pl.BlockSpec((tm, kr), lambda i: (i, 0)),
                  pl.BlockSpec((tm, ka), lambda i: (i, 0)),
                  pl.BlockSpec((tm, d), lambda i: (i, 0)),
                  pl.BlockSpec((None, N_MOD, d), lambda i: (i // per_b, 0, 0)),
                  pl.BlockSpec((1, ka), lambda i: (0, 0)),
                  pl.BlockSpec((kr + ka, d), lambda i: (0, 0))],
        out_specs=pl.BlockSpec((tm, d), lambda i: (i, 0)),
        out_shape=jax.ShapeDtypeStruct((t, d), F32),
        compiler_params=_cparams(("parallel",), 48),
        name="oproj",
    )(ret_out, attn, x2d, mod, ow, w_o)


def _ffn_kernel(x_ref, mod_ref, n2_ref, wg_ref, wu_ref, wd_ref, fw_ref, o_ref, h_ref):
    f = pl.program_id(1)
    last = pl.num_programs(1) - 1

    def swiglu_down(h):
        g = jnp.dot(h, wg_ref[...], preferred_element_type=F32)
        u = jnp.dot(h, wu_ref[...], preferred_element_type=F32)
        a = (_silu(g) * u).astype(BF16)
        return jnp.dot(a, wd_ref[...], preferred_element_type=F32)

    @pl.when(f == 0)
    def _():
        h = (_rms(x_ref[...], n2_ref[...]) * (1.0 + mod_ref[4:5, :]) + mod_ref[3:4, :]).astype(BF16)
        h_ref[...] = h
        o_ref[...] = swiglu_down(h)

    @pl.when((f > 0) & (f < last))
    def _():
        o_ref[...] += swiglu_down(h_ref[...])

    @pl.when(f == last)
    def _():
        x2 = x_ref[...] + mod_ref[5:6, :] * (o_ref[...] + swiglu_down(h_ref[...]))
        o_ref[...] = _rms(x2, fw_ref[...])


def _ffn(x2d, mod, n2w, wg, wu, wd, fw, seq):
    t, d = x2d.shape
    dff = wg.shape[1]
    tm = min(1024, seq)
    tf = 512
    per_b = seq // tm
    row = pl.BlockSpec((1, d), lambda i, f: (0, 0))
    return pl.pallas_call(
        _ffn_kernel,
        grid=(t // tm, dff // tf),
        in_specs=[pl.BlockSpec((tm, d), lambda i, f: (i, 0)),
                  pl.BlockSpec((None, N_MOD, d), lambda i, f: (i // per_b, 0, 0)),
                  row,
                  pl.BlockSpec((d, tf), lambda i, f: (0, f)),
                  pl.BlockSpec((d, tf), lambda i, f: (0, f)),
                  pl.BlockSpec((tf, d), lambda i, f: (f, 0)),
                  row],
        out_specs=pl.BlockSpec((tm, d), lambda i, f: (i, 0)),
        out_shape=jax.ShapeDtypeStruct((t, d), F32),
        scratch_shapes=[pltpu.VMEM((tm, d), BF16)],
        compiler_params=_cparams(("parallel", "arbitrary"), 60),
        name="ffn",
    )(x2d, mod, n2w, wg, wu, wd, fw)


def _rope_freq_row():
    fr = ROPE_BASE ** (-jnp.arange(0, RET_HEAD_DIM, 2, dtype=F32) / RET_HEAD_DIM)
    fm = ROPE_BASE ** (-jnp.arange(0, MLA_ROPE_DIM, 2, dtype=F32) / MLA_ROPE_DIM)
    pad = jnp.zeros((LANES - fr.shape[0] - fm.shape[0],), F32)
    return jnp.concatenate([fr, fm, pad]).reshape(1, LANES)


def kernel(x, c, positions, ada_w, ada_b, norm1_w, w_in, ret_decay, ret_gn_w, ret_gn_b, mla_q_norm_w, w_uq,
           mla_kv_norm_w, w_ukv, mla_out_w, w_o, norm2_w, w_gate, w_up, w_down, final_norm_w):
    bsz, seq, d = x.shape
    t = bsz * seq
    assert ada_w.shape[0] == 1, "the final norm is fused into the (single) layer's FFN kernel"
    ret_w = RET_HEADS * RET_HEAD_DIM
    n_main = 4 * ret_w + MLA_Q_RANK + MLA_KV_RANK

    tabs = _rope_tables(positions.reshape(t, 1), _rope_freq_row())
    xc = x.reshape(t, d)
    drop_depth = lambda w: w.reshape(w.shape[1:])
    mod = _adaln(c, drop_depth(ada_w), ada_b[0]).reshape(bsz, N_MOD, d)

    w_all = drop_depth(w_in).astype(BF16)
    w_rope = jnp.pad(w_all[:, n_main:], ((0, 0), (0, LANES - MLA_ROPE_DIM)))
    proj, k_rope = _inproj(xc, mod, norm1_w[0].reshape(1, d), w_all, n_main, w_rope, tabs, seq)

    wq = w_uq[0].reshape(MLA_Q_RANK, MLA_HEADS, MLA_QK_DIM)
    wq = jnp.pad(wq, ((0, 0), (0, 0), (0, MLA_QK_PAD - MLA_QK_DIM)))
    w_uq_p = wq.reshape(MLA_Q_RANK, MLA_HEADS * MLA_QK_PAD).astype(BF16)
    wkv = w_ukv[0].reshape(MLA_KV_RANK, MLA_HEADS, MLA_NOPE_DIM + MLA_V_DIM)
    w_ukv_p = jnp.concatenate(
        [wkv[:, :, :MLA_NOPE_DIM].reshape(MLA_KV_RANK, -1), wkv[:, :, MLA_NOPE_DIM:].reshape(MLA_KV_RANK, -1)],
        axis=1).astype(BF16)
    q_m, kv_m = _mla_up(proj, mla_q_norm_w[0].reshape(1, -1), mla_kv_norm_w[0].reshape(1, -1), w_uq_p, w_ukv_p,
                        tabs[2:], seq)
    attn = _attention(q_m, kv_m, k_rope, bsz, seq)

    log_gamma = -jnp.exp(ret_decay[0].astype(F32))
    ret_out = _retention(log_gamma, proj, ret_gn_w[0].reshape(1, -1), ret_gn_b[0].reshape(1, -1), bsz, seq)

    x1 = _oproj(ret_out, attn, xc, mod, mla_out_w[0].reshape(1, -1), drop_depth(w_o).astype(BF16), seq)
    out = _ffn(x1, mod, norm2_w[0].reshape(1, d), drop_depth(w_gate).astype(BF16), drop_depth(w_up).astype(BF16),
               drop_depth(w_down).astype(BF16), final_norm_w.reshape(1, d), seq)
    return out.reshape(bsz, seq, d)
```
